```python
import jax, jax.numpy as jnp
from jax import lax
import numpy as np

D_MODEL = 1024
BATCH = 8
SEQ = 2048
DEPTH = 4
DEC_BATCH = 128
DEC_SEQ = 4
PAST_LEN = 16384
PAGE_SIZE = 128

N_MIXERS = 2
N_A_LAYERS = (DEPTH + 1) // 2
N_B_LAYERS = DEPTH // 2
CHUNK = 128
D_A = 2 * D_MODEL
N_GROUPS_A = 8
D_GROUP_A = D_A // N_GROUPS_A
D_B = D_MODEL
CONV_W = 3
PEER_HEADS = 8
N_KEYS = 128
N_EXPERTS = N_KEYS * N_KEYS
D_QUERY = 256
D_HALF = D_QUERY // 2
TOPK_HALF = 16
TOPK = 16
PEER_BLOCK = 256
EPS = 1e-6

kernel_name = "hybrid_gmlp_shortconv_peer_step"


def rms_norm(x, g):
    xf = x.astype(jnp.float32)
    y = xf * lax.rsqrt(jnp.mean(xf * xf, axis=-1, keepdims=True) + EPS)
    return (y * g.astype(jnp.float32)).astype(x.dtype)


def layer_norm(x, g, b):
    xf = x.astype(jnp.float32)
    mu = jnp.mean(xf, axis=-1, keepdims=True)
    var = jnp.mean(jnp.square(xf - mu), axis=-1, keepdims=True)
    y = (xf - mu) * lax.rsqrt(var + EPS)
    return (y * g.astype(jnp.float32) + b.astype(jnp.float32)).astype(x.dtype)


def chunk_mlp_mixer(x, w_in, ln_g, ln_b, w_s, b_s, w_out):
    bsz, s_len, _ = x.shape
    L = min(s_len, CHUNK)
    C = s_len // L
    z = jax.nn.gelu(x @ w_in)
    u, v = jnp.split(z, 2, axis=-1)
    v = layer_norm(v, ln_g, ln_b)
    mask = jnp.tril(jnp.ones((L, L), dtype=bool))
    ws = jnp.where(mask[None], w_s[:, :L, :L], 0).astype(v.dtype)
    vc = v.reshape(bsz, C, L, N_GROUPS_A, D_GROUP_A)
    bias = jnp.transpose(b_s[:, :L]).astype(v.dtype)
    mixed = jnp.einsum('gij,bcjgd->bcigd', ws, vc) + bias[None, None, :, :, None]
    y = (u * mixed.reshape(bsz, s_len, D_A)) @ w_out
    return y, v


def short_conv_mixer(x, buf, w_in, conv_w, w_out):
    s_len = x.shape[1]
    b_gate, c_gate, h = jnp.split(x @ w_in, 3, axis=-1)
    z = c_gate * h
    zp = jnp.concatenate([buf.astype(z.dtype), z], axis=1)
    conv = sum(conv_w[k] * zp[:, k:k + s_len] for k in range(CONV_W))
    y = (b_gate * conv) @ w_out
    return y, zp[:, s_len:]


def peer_ffn(h, w_query, sub_keys, expert_u, expert_v):
    n_tok = h.shape[0]
    n_blk = -(-n_tok // PEER_BLOCK)
    hp = jnp.pad(h, ((0, n_blk * PEER_BLOCK - n_tok), (0, 0))).reshape(n_blk, PEER_BLOCK, D_MODEL)

    def block(xb):
        q = (xb @ w_query).reshape(PEER_BLOCK, PEER_HEADS, 2, D_HALF)
        s = jnp.einsum('thpd,hpkd->thpk', q.astype(jnp.float32), sub_keys.astype(jnp.float32))
        sv, si = lax.top_k(s, TOPK_HALF)
        cand = sv[:, :, 0, :, None] + sv[:, :, 1, None, :]
        cv, ci = lax.top_k(cand.reshape(PEER_BLOCK, PEER_HEADS, TOPK_HALF * TOPK_HALF), TOPK)
        i1 = jnp.take_along_axis(si[:, :, 0], ci // TOPK_HALF, axis=-1)
        i2 = jnp.take_along_axis(si[:, :, 1], ci % TOPK_HALF, axis=-1)
        idx = i1 * N_KEYS + i2
        g = jax.nn.softmax(cv, axis=-1)
        u_sel = expert_u[idx]
        a = jax.nn.gelu(jnp.einsum('thkd,td->thk', u_sel, xb))
        return jnp.einsum('thk,thkd->td', g.astype(a.dtype) * a, expert_v[idx])

    out = lax.map(block, hp)
    return out.reshape(n_blk * PEER_BLOCK, D_MODEL)[:n_tok]


def setup_inputs(seed: int = 0) -> dict:
    key = jax.random.key(seed)
    ks = jax.random.split(key, 20)
    f32 = jnp.float32
    nrm = lambda k, shape, scale: jax.random.normal(k, shape, f32) * scale
    return {
        "x_prompt": nrm(ks[0], (BATCH, SEQ, D_MODEL), 1.0),
        "x_sample": nrm(ks[1], (DEC_BATCH, DEC_SEQ, D_MODEL), 1.0),
        "state_conv": nrm(ks[2], (N_B_LAYERS, DEC_BATCH, CONV_W - 1, D_B), 0.5),
        "norm_mix": 1.0 + nrm(ks[3], (DEPTH, D_MODEL), 0.02),
        "norm_ffn": 1.0 + nrm(ks[4], (DEPTH, D_MODEL), 0.02),
        "norm_final": 1.0 + nrm(ks[5], (D_MODEL,), 0.02),
        "a_w_in": nrm(ks[6], (N_A_LAYERS, D_MODEL, 2 * D_A), D_MODEL ** -0.5),
        "a_ln_g": 1.0 + nrm(ks[7], (N_A_LAYERS, D_A), 0.02),
        "a_ln_b": nrm(ks[8], (N_A_LAYERS, D_A), 0.02),
        "a_w_s": nrm(ks[9], (N_A_LAYERS, N_GROUPS_A, CHUNK, CHUNK), CHUNK ** -0.5),
        "a_b_s": 1.0 + nrm(ks[10], (N_A_LAYERS, N_GROUPS_A, CHUNK), 0.02),
        "a_w_out": nrm(ks[11], (N_A_LAYERS, D_A, D_MODEL), D_A ** -0.5),
        "b_w_in": nrm(ks[12], (N_B_LAYERS, D_MODEL, 3 * D_B), D_MODEL ** -0.5),
        "b_conv_w": nrm(ks[13], (N_B_LAYERS, CONV_W, D_B), CONV_W ** -0.5),
        "b_w_out": nrm(ks[14], (N_B_LAYERS, D_B, D_MODEL), D_B ** -0.5),
        "peer_w_query": nrm(ks[15], (DEPTH, D_MODEL, PEER_HEADS * D_QUERY), D_MODEL ** -0.5),
        "peer_sub_keys": nrm(ks[16], (DEPTH, PEER_HEADS, 2, N_KEYS, D_HALF), D_HALF ** -0.5),
        "peer_expert_u": nrm(ks[17], (DEPTH, N_EXPERTS, D_MODEL), D_MODEL ** -0.5),
        "peer_expert_v": nrm(ks[18], (DEPTH, N_EXPERTS, D_MODEL), 0.5 * PEER_HEADS ** -0.5),
    }


def reference(x_prompt, x_sample, state_conv, norm_mix, norm_ffn, norm_final,
              a_w_in, a_ln_g, a_ln_b, a_w_s, a_b_s, a_w_out,
              b_w_in, b_conv_w, b_w_out,
              peer_w_query, peer_sub_keys, peer_expert_u, peer_expert_v):
    xp, xs = x_prompt, x_sample
    n_prompt_tok = BATCH * SEQ
    v_rows_sample, conv_prompt, conv_sample = [], [], []
    for i in range(DEPTH):
        j = i // N_MIXERS
        hp = rms_norm(xp, norm_mix[i])
        hs = rms_norm(xs, norm_mix[i])
        if i % N_MIXERS == 0:
            yp, _ = chunk_mlp_mixer(hp, a_w_in[j], a_ln_g[j], a_ln_b[j], a_w_s[j], a_b_s[j], a_w_out[j])
            ys, vs = chunk_mlp_mixer(hs, a_w_in[j], a_ln_g[j], a_ln_b[j], a_w_s[j], a_b_s[j], a_w_out[j])
            v_rows_sample.append(vs)
        else:
            zero_buf = jnp.zeros((BATCH, CONV_W - 1, D_B), hp.dtype)
            yp, bp = short_conv_mixer(hp, zero_buf, b_w_in[j], b_conv_w[j], b_w_out[j])
            ys, bs = short_conv_mixer(hs, state_conv[j], b_w_in[j], b_conv_w[j], b_w_out[j])
            conv_prompt.append(bp)
            conv_sample.append(bs)
        xp = xp + yp
        xs = xs + ys
        tok = jnp.concatenate([rms_norm(xp, norm_ffn[i]).reshape(-1, D_MODEL),
                               rms_norm(xs, norm_ffn[i]).reshape(-1, D_MODEL)], axis=0)
        f = peer_ffn(tok, peer_w_query[i], peer_sub_keys[i], peer_expert_u[i], peer_expert_v[i])
        xp = xp + f[:n_prompt_tok].reshape(xp.shape)
        xs = xs + f[n_prompt_tok:].reshape(xs.shape)
    y_prompt = rms_norm(xp, norm_final)
    y_sample = rms_norm(xs, norm_final)
    return (y_prompt, y_sample, jnp.stack(v_rows_sample), jnp.stack(conv_prompt), jnp.stack(conv_sample))
```

```python
import functools

import jax
import jax.numpy as jnp
from jax import lax
from jax.experimental import pallas as pl
from jax.experimental.pallas import tpu as pltpu

F32 = jnp.float32
BF16 = jnp.bfloat16

EPS = 1e-6
CHUNK = 128
N_GROUPS_A = 8
TOPK_HALF = 16
TOPK = 16
LANES = 128
SUBLANES = 8
TOKEN_BLOCK = 512
KEYS_PER_STEP = SUBLANES
VMEM_LIMIT = 56 * 1024 * 1024
NEG_INF = float("-inf")


def _rms(x, g):
    return x * lax.rsqrt(jnp.mean(x * x, axis=-1, keepdims=True) + EPS) * g


def _params(n_axes):
    return pltpu.CompilerParams(dimension_semantics=("arbitrary",) * n_axes,
                                vmem_limit_bytes=VMEM_LIMIT)


def _mixer_a_kernel(x_ref, g_ref, win_ref, lng_ref, lnb_ref, ws_ref, bias_ref, wout_ref,
                    xo_ref, v_ref, gated_ref):
    tb = x_ref.shape[0]
    d_a = lng_ref.shape[1]
    d_g = d_a // N_GROUPS_A
    x = x_ref[...]
    h = _rms(x, g_ref[...]).astype(BF16)
    z = jax.nn.gelu(jnp.dot(h, win_ref[...], preferred_element_type=F32))
    v = z[:, d_a:]
    mu = jnp.mean(v, axis=-1, keepdims=True)
    vc = v - mu
    var = jnp.mean(vc * vc, axis=-1, keepdims=True)
    vn = vc * lax.rsqrt(var + EPS) * lng_ref[...] + lnb_ref[...]
    v_ref[...] = vn
    vb = vn.astype(BF16)
    for c in range(tb // CHUNK):
        rows = slice(c * CHUNK, (c + 1) * CHUNK)
        for g in range(N_GROUPS_A):
            cols = slice(g * d_g, (g + 1) * d_g)
            mixed = jnp.dot(ws_ref[0, g], vb[rows, cols], preferred_element_type=F32)
            mixed = mixed + bias_ref[0, :, cols]
            gated_ref[rows, cols] = (z[rows, cols] * mixed).astype(BF16)
    y = jnp.dot(gated_ref[...], wout_ref[...], preferred_element_type=F32)
    xo_ref[...] = x + y


def _mixer_a(x, g, w_in, ln_g, ln_b, ws2, bias2, w_out, n_prompt_blocks):
    t, d = x.shape
    tb = TOKEN_BLOCK
    d_a = ln_g.shape[0]
    nb = t // tb
    n_sample_blocks = nb - n_prompt_blocks
    is_sample = lambda i: jnp.where(i >= n_prompt_blocks, 1, 0)
    const = lambda i: (0, 0)
    return pl.pallas_call(
        _mixer_a_kernel,
        grid=(nb,),
        in_specs=[
            pl.BlockSpec((tb, d), lambda i: (i, 0)),
            pl.BlockSpec((1, d), const),
            pl.BlockSpec((d, 2 * d_a), const),
            pl.BlockSpec((1, d_a), const),
            pl.BlockSpec((1, d_a), const),
            pl.BlockSpec((1, N_GROUPS_A, CHUNK, CHUNK), lambda i: (is_sample(i), 0, 0, 0)),
            pl.BlockSpec((1, CHUNK, d_a), lambda i: (is_sample(i), 0, 0)),
            pl.BlockSpec((d_a, d), const),
        ],
        out_specs=[
            pl.BlockSpec((tb, d), lambda i: (i, 0)),
            pl.BlockSpec((tb, d_a), lambda i: (jnp.maximum(i - n_prompt_blocks, 0), 0)),
        ],
        out_shape=[
            jax.ShapeDtypeStruct((t, d), F32),
            jax.ShapeDtypeStruct((n_sample_blocks * tb, d_a), F32),
        ],
        scratch_shapes=[pltpu.VMEM((tb, d_a), BF16)],
        compiler_params=_params(1),
        name="mixer_a",
    )(x, g.reshape(1, d), w_in, ln_g.reshape(1, d_a), ln_b.reshape(1, d_a), ws2, bias2, w_out)


def _mixer_b_kernel(x_ref, g_ref, win_ref, cw_ref, wout_ref, h1_ref, h2_ref,
                    xo_ref, zlast_ref, zs_ref, zext_ref, *, n_prompt_blocks, blocks_per_seq,
                    sample_len):
    i = pl.program_id(0)
    tb = x_ref.shape[0]
    d_b = wout_ref.shape[0]
    halo = SUBLANES
    x = x_ref[...]
    h = _rms(x, g_ref[...]).astype(BF16)
    bcz = jnp.dot(h, win_ref[...], preferred_element_type=F32)
    b_gate = bcz[:, :d_b]
    z = bcz[:, d_b:2 * d_b] * bcz[:, 2 * d_b:]
    is_sample = i >= n_prompt_blocks

    @pl.when(jnp.logical_or(is_sample, i % blocks_per_seq == 0))
    def _():
        zext_ref[0:halo, :] = jnp.zeros((halo, d_b), F32)

    zext_ref[halo:, :] = z
    z1 = zext_ref[halo - 1:halo - 1 + tb, :]
    z2 = zext_ref[halo - 2:halo - 2 + tb, :]
    pos = lax.broadcasted_iota(jnp.int32, (tb, 1), 0) % sample_len
    z1 = jnp.where(jnp.logical_and(is_sample, pos < 1), h1_ref[...], z1)
    z2 = jnp.where(jnp.logical_and(is_sample, pos < 2), h2_ref[...], z2)
    conv = cw_ref[0:1, :] * z2 + cw_ref[1:2, :] * z1 + cw_ref[2:3, :] * z
    y = jnp.dot((b_gate * conv).astype(BF16), wout_ref[...], preferred_element_type=F32)
    xo_ref[...] = x + y
    tail = z[tb - halo:, :]
    zlast_ref[0] = tail
    zs_ref[...] = z
    zext_ref[0:halo, :] = tail


def _mixer_b(x, g, w_in, conv_w, w_out, h1, h2, n_prompt_blocks, blocks_per_seq, sample_len):
    t, d = x.shape
    tb = TOKEN_BLOCK
    d_b = w_out.shape[0]
    nb = t // tb
    n_sample_blocks = nb - n_prompt_blocks
    const = lambda i: (0, 0)
    sample_blk = lambda i: (jnp.maximum(i - n_prompt_blocks, 0), 0)
    kern = functools.partial(_mixer_b_kernel, n_prompt_blocks=n_prompt_blocks,
                             blocks_per_seq=blocks_per_seq, sample_len=sample_len)
    return pl.pallas_call(
        kern,
        grid=(nb,),
        in_specs=[
            pl.BlockSpec((tb, d), lambda i: (i, 0)),
            pl.BlockSpec((1, d), const),
            pl.BlockSpec((d, 3 * d_b), const),
            pl.BlockSpec((conv_w.shape[0], d_b), const),
            pl.BlockSpec((d_b, d), const),
            pl.BlockSpec((tb, d_b), sample_blk),
            pl.BlockSpec((tb, d_b), sample_blk),
        ],
        out_specs=[
            pl.BlockSpec((tb, d), lambda i: (i, 0)),
            pl.BlockSpec((1, SUBLANES, d_b), lambda i: (i, 0, 0)),
            pl.BlockSpec((tb, d_b), sample_blk),
        ],
        out_shape=[
            jax.ShapeDtypeStruct((t, d), F32),
            jax.ShapeDtypeStruct((nb, SUBLANES, d_b), F32),
            jax.ShapeDtypeStruct((n_sample_blocks * tb, d_b), F32),
        ],
        scratch_shapes=[pltpu.VMEM((tb + SUBLANES, d_b), F32)],
        compiler_params=_params(1),
        name="mixer_b",
    )(x, g.reshape(1, d), w_in, conv_w, w_out, h1, h2)


def _top_values(s, n):
    out = []
    cur = s
    for _ in range(n):
        m = jnp.max(cur, axis=0, keepdims=True)
        out.append(m)
        cur = jnp.where(cur == m, NEG_INF, cur)
    return out


def _peer_kernel(x_ref, g_ref, wqt_ref, keys_ref, u_ref, v_ref, xo_ref,
                 tokt_ref, qt_ref, s1_ref, s2_ref, e2_ref, c1_ref, tau_ref, sv1_ref, sv2_ref,
                 cand_ref, at_ref, mt_ref, acc_ref):
    kg = pl.program_id(1)
    tb = x_ref.shape[0]
    n_heads = keys_ref.shape[0]
    n_keys = keys_ref.shape[2]
    d_half = keys_ref.shape[3]
    n_lane_tiles = tb // LANES

    @pl.when(kg == 0)
    def _route():
        tok = _rms(x_ref[...], g_ref[...])
        tokt_ref[...] = tok.T.astype(BF16)
        qt_ref[...] = jnp.dot(wqt_ref[...], tokt_ref[...], preferred_element_type=F32)
        acc_ref[...] = jnp.zeros_like(acc_ref)

        def head(h, carry):
            for p, s_ref in ((0, s1_ref), (1, s2_ref)):
                row0 = pl.multiple_of((2 * h + p) * d_half, d_half)
                s_ref[h] = jnp.dot(keys_ref[h, p], qt_ref[pl.ds(row0, d_half), :],
                                   preferred_element_type=F32, precision=lax.Precision.HIGHEST)
            for j in range(n_lane_tiles):
                lanes = slice(j * LANES, (j + 1) * LANES)
                s1 = s1_ref[h, :, lanes]
                s2 = s2_ref[h, :, lanes]
                for a, m in enumerate(_top_values(s1, TOPK_HALF)):
                    sv1_ref[a:a + 1, :] = m
                for a, m in enumerate(_top_values(s2, TOPK_HALF)):
                    sv2_ref[a:a + 1, :] = m
                half = TOPK_HALF // 2
                cand_ref[0:TOPK_HALF, :] = sv1_ref[0:1, :] + sv2_ref[...]
                for a in range(1, half):
                    cand_ref[TOPK_HALF + (a - 1) * half:TOPK_HALF + a * half, :] = (
                        sv1_ref[a:a + 1, :] + sv2_ref[0:half, :])
                cand_ref[TOPK_HALF + (half - 1) * half:, :] = sv1_ref[half:, :] + sv2_ref[0:1, :]
                best = _top_values(cand_ref[...], TOPK)
                zsum = jnp.zeros_like(best[0])
                for m in best:
                    zsum = zsum + jnp.exp(m - best[0])
                tau_ref[h, :, lanes] = jnp.broadcast_to(best[-1], (SUBLANES, LANES))
                c1_ref[h, :, lanes] = jnp.exp(s1 - sv1_ref[0:1, :]) / zsum
                e2_ref[h, :, lanes] = jnp.exp(s2 - sv2_ref[0:1, :])
            return carry

        lax.fori_loop(0, n_heads, head, 0)

    at_ref[...] = jnp.dot(u_ref[...], tokt_ref[...], preferred_element_type=F32)
    rows_per_tile = 8 * SUBLANES
    key_rows = pl.ds(pl.multiple_of(kg * KEYS_PER_STEP, KEYS_PER_STEP), KEYS_PER_STEP)
    for ii in range(KEYS_PER_STEP):
        for j in range(n_lane_tiles):
            lanes = slice(j * LANES, (j + 1) * LANES)
            for r in range(n_keys // rows_per_tile):
                rows = slice(r * rows_per_tile, (r + 1) * rows_per_tile)
                w = jnp.zeros((rows_per_tile, LANES), F32)
                for h in range(n_heads):
                    s1row = s1_ref[h, key_rows, lanes][ii:ii + 1, :]
                    c1row = c1_ref[h, key_rows, lanes][ii:ii + 1, :]
                    pair = s2_ref[h, rows, lanes] + s1row
                    w = w + jnp.where(pair >= tau_ref[h, 0:1, lanes], e2_ref[h, rows, lanes], 0.0) * c1row
                arow = slice(ii * n_keys + r * rows_per_tile, ii * n_keys + (r + 1) * rows_per_tile)
                mt_ref[arow, lanes] = (w * jax.nn.gelu(at_ref[arow, lanes])).astype(BF16)
    acc_ref[...] += lax.dot_general(mt_ref[...], v_ref[...], (((0,), (0,)), ((), ())),
                                    preferred_element_type=F32)

    @pl.when(kg == pl.num_programs(1) - 1)
    def _finish():
        xo_ref[...] = x_ref[...] + acc_ref[...]


def _peer(x, g, wq_t, keys, u_tab, v_tab):
    t, d = x.shape
    tb = TOKEN_BLOCK
    n_heads, _, n_keys, d_half = keys.shape
    n_experts = u_tab.shape[0]
    eb = KEYS_PER_STEP * n_keys
    grid = (t // tb, n_experts // eb)
    return pl.pallas_call(
        _peer_kernel,
        grid=grid,
        in_specs=[
            pl.BlockSpec((tb, d), lambda i, k: (i, 0)),
            pl.BlockSpec((1, d), lambda i, k: (0, 0)),
            pl.BlockSpec(wq_t.shape, lambda i, k: (0, 0)),
            pl.BlockSpec(keys.shape, lambda i, k: (0, 0, 0, 0)),
            pl.BlockSpec((eb, d), lambda i, k: (k, 0)),
            pl.BlockSpec((eb, d), lambda i, k: (k, 0)),
        ],
        out_specs=pl.BlockSpec((tb, d), lambda i, k: (i, 0)),
        out_shape=jax.ShapeDtypeStruct((t, d), F32),
        scratch_shapes=[
            pltpu.VMEM((d, tb), BF16),
            pltpu.VMEM((wq_t.shape[0], tb), F32),
            pltpu.VMEM((n_heads, n_keys, tb), F32),
            pltpu.VMEM((n_heads, n_keys, tb), F32),
            pltpu.VMEM((n_heads, n_keys, tb), F32),
            pltpu.VMEM((n_heads, n_keys, tb), F32),
            pltpu.VMEM((n_heads, SUBLANES, tb), F32),
            pltpu.VMEM((TOPK_HALF, LANES), F32),
            pltpu.VMEM((TOPK_HALF, LANES), F32),
            pltpu.VMEM((TOPK_HALF + (TOPK_HALF // 2) ** 2, LANES), F32),
            pltpu.VMEM((eb, tb), F32),
            pltpu.VMEM((eb, tb), BF16),
            pltpu.VMEM((tb, d), F32),
        ],
        compiler_params=_params(2),
        name="peer",
    )(x, g.reshape(1, d), wq_t, keys, u_tab, v_tab)


def _final_norm_kernel(x_ref, g_ref, o_ref):
    o_ref[...] = _rms(x_ref[...], g_ref[...])


def _final_norm(x, g):
    t, d = x.shape
    tb = TOKEN_BLOCK
    return pl.pallas_call(
        _final_norm_kernel,
        grid=(t // tb,),
        in_specs=[pl.BlockSpec((tb, d), lambda i: (i, 0)), pl.BlockSpec((1, d), lambda i: (0, 0))],
        out_specs=pl.BlockSpec((tb, d), lambda i: (i, 0)),
        out_shape=jax.ShapeDtypeStruct((t, d), F32),
        compiler_params=_params(1),
        name="final_norm",
    )(x, g.reshape(1, d))


def _spatial_weights(w_s, b_s, sample_len, d_a):
    n_g = w_s.shape[0]
    d_g = d_a // n_g
    tril = lambda n: jnp.tril(jnp.ones((n, n), dtype=bool))
    ws_p = jnp.where(tril(CHUNK)[None], w_s, 0)
    ws4 = jnp.where(tril(sample_len)[None], w_s[:, :sample_len, :sample_len], 0)
    reps = CHUNK // sample_len
    eye = jnp.eye(reps, dtype=w_s.dtype)
    ws_s = (eye[None, :, None, :, None] * ws4[:, None, :, None, :]).reshape(n_g, CHUNK, CHUNK)
    ws2 = jnp.stack([ws_p, ws_s]).astype(BF16)
    bias_p = jnp.repeat(jnp.transpose(b_s), d_g, axis=1)
    bias_s = jnp.repeat(jnp.tile(jnp.transpose(b_s[:, :sample_len]), (reps, 1)), d_g, axis=1)
    return ws2, jnp.stack([bias_p, bias_s]).astype(F32)


def kernel(x_prompt, x_sample, state_conv, norm_mix, norm_ffn, norm_final, a_w_in, a_ln_g, a_ln_b,
           a_w_s, a_b_s, a_w_out, b_w_in, b_conv_w, b_w_out, peer_w_query, peer_sub_keys,
           peer_expert_u, peer_expert_v):
    batch, seq, d = x_prompt.shape
    n_seq_s, sample_len, _ = x_sample.shape
    depth = norm_mix.shape[0]
    d_a = a_ln_g.shape[1]
    d_b = b_w_out.shape[1]
    conv_taps = b_conv_w.shape[1]
    tb = TOKEN_BLOCK
    n_prompt = batch * seq
    n_sample = n_seq_s * sample_len
    assert seq % tb == 0 and n_sample % tb == 0 and tb % CHUNK == 0
    assert CHUNK % sample_len == 0 and sample_len >= conv_taps - 1 and conv_taps == 3
    assert peer_sub_keys.shape[3] % (8 * SUBLANES) == 0
    n_prompt_blocks = n_prompt // tb

    x = jnp.concatenate([x_prompt.reshape(n_prompt, d), x_sample.reshape(n_sample, d)], axis=0)
    v_rows, conv_prompt, conv_sample = [], [], []
    for i in range(depth):
        j = i // 2
        if i % 2 == 0:
            ws2, bias2 = _spatial_weights(a_w_s[j], a_b_s[j], sample_len, d_a)
            x, v = _mixer_a(x, norm_mix[i], a_w_in[j].astype(BF16), a_ln_g[j], a_ln_b[j], ws2, bias2,
                            a_w_out[j].astype(BF16), n_prompt_blocks)
            v_rows.append(v.reshape(n_seq_s, sample_len, d_a))
        else:
            buf = state_conv[j]
            pad = lambda rows: jnp.concatenate(
                [rows, jnp.zeros((n_seq_s, sample_len - rows.shape[1], d_b), F32)], axis=1
            ).reshape(n_sample, d_b)
            h1 = pad(buf[:, 1:2])
            h2 = pad(buf[:, 0:2])
            x, zlast, zs = _mixer_b(x, norm_mix[i], b_w_in[j].astype(BF16), b_conv_w[j],
                                    b_w_out[j].astype(BF16), h1, h2, n_prompt_blocks, seq // tb,
                                    sample_len)
            blocks_per_seq = seq // tb
            conv_prompt.append(zlast[blocks_per_seq - 1:n_prompt_blocks:blocks_per_seq, SUBLANES - 2:])
            conv_sample.append(zs.reshape(n_seq_s, sample_len, d_b)[:, sample_len - 2:])
        x = _peer(x, norm_ffn[i], jnp.transpose(peer_w_query[i]).astype(BF16), peer_sub_keys[i],
                  peer_expert_u[i].astype(BF16), peer_expert_v[i].astype(BF16))
    y = _final_norm(x, norm_final)
    return (y[:n_prompt].reshape(batch, seq, d), y[n_prompt:].reshape(n_seq_s, sample_len, d),
            jnp.stack(v_rows), jnp.stack(conv_prompt), jnp.stack(conv_sample))
```

```python
import functools

import jax
import jax.numpy as jnp
from jax import lax
from jax.experimental import pallas as pl
from jax.experimental.pallas import tpu as pltpu

F32 = jnp.float32
BF16 = jnp.bfloat16

EPS = 1e-6
CHUNK = 128
N_GROUPS_A = 8
TOPK_HALF = 16
TOPK = 16
LANES = 128
SUBLANES = 8
TOKEN_BLOCK = 512
KEYS_PER_STEP = SUBLANES
GATE_ROWS = 2 * SUBLANES
GATE_KEYS = KEYS_PER_STEP
A_ROW_CHUNKS = 8
C_TOKEN_CHUNK = 512
VMEM_LIMIT = 56 * 1024 * 1024
GELU_K0 = 0.7978845608028654
GELU_K1 = GELU_K0 * 0.044715
NEG_INF = float("-inf")


def _rms(x, g):
    return x * lax.rsqrt(jnp.mean(x * x, axis=-1, keepdims=True) + EPS) * g


def _params(n_axes):
    return pltpu.CompilerParams(dimension_semantics=("arbitrary",) * n_axes,
                                vmem_limit_bytes=VMEM_LIMIT)


def _mixer_a_kernel(x_ref, g_ref, win_ref, lng_ref, lnb_ref, ws_ref, bias_ref, wout_ref,
                    xo_ref, v_ref, gated_ref):
    tb = x_ref.shape[0]
    d_a = lng_ref.shape[1]
    d_g = d_a // N_GROUPS_A
    x = x_ref[...]
    h = _rms(x, g_ref[...]).astype(BF16)
    z = jax.nn.gelu(jnp.dot(h, win_ref[...], preferred_element_type=F32))
    v = z[:, d_a:]
    mu = jnp.mean(v, axis=-1, keepdims=True)
    vc = v - mu
    var = jnp.mean(vc * vc, axis=-1, keepdims=True)
    vn = vc * lax.rsqrt(var + EPS) * lng_ref[...] + lnb_ref[...]
    v_ref[...] = vn
    vb = vn.astype(BF16)
    for c in range(tb // CHUNK):
        rows = slice(c * CHUNK, (c + 1) * CHUNK)
        for g in range(N_GROUPS_A):
            cols = slice(g * d_g, (g + 1) * d_g)
            mixed = jnp.dot(ws_ref[0, g], vb[rows, cols], preferred_element_type=F32)
            mixed = mixed + bias_ref[0, :, cols]
            gated_ref[rows, cols] = (z[rows, cols] * mixed).astype(BF16)
    y = jnp.dot(gated_ref[...], wout_ref[...], preferred_element_type=F32)
    xo_ref[...] = x + y


def _mixer_a(x, g, w_in, ln_g, ln_b, ws2, bias2, w_out, n_prompt_blocks):
    t, d = x.shape
    tb = TOKEN_BLOCK
    d_a = ln_g.shape[0]
    nb = t // tb
    n_sample_blocks = nb - n_prompt_blocks
    is_sample = lambda i: jnp.where(i >= n_prompt_blocks, 1, 0)
    const = lambda i: (0, 0)
    return pl.pallas_call(
        _mixer_a_kernel,
        grid=(nb,),
        in_specs=[
            pl.BlockSpec((tb, d), lambda i: (i, 0)),
            pl.BlockSpec((1, d), const),
            pl.BlockSpec((d, 2 * d_a), const),
            pl.BlockSpec((1, d_a), const),
            pl.BlockSpec((1, d_a), const),
            pl.BlockSpec((1, N_GROUPS_A, CHUNK, CHUNK), lambda i: (is_sample(i), 0, 0, 0)),
            pl.BlockSpec((1, CHUNK, d_a), lambda i: (is_sample(i), 0, 0)),
            pl.BlockSpec((d_a, d), const),
        ],
        out_specs=[
            pl.BlockSpec((tb, d), lambda i: (i, 0)),
            pl.BlockSpec((tb, d_a), lambda i: (jnp.maximum(i - n_prompt_blocks, 0), 0)),
        ],
        out_shape=[
            jax.ShapeDtypeStruct((t, d), F32),
            jax.ShapeDtypeStruct((n_sample_blocks * tb, d_a), F32),
        ],
        scratch_shapes=[pltpu.VMEM((tb, d_a), BF16)],
        compiler_params=_params(1),
        name="mixer_a",
    )(x, g.reshape(1, d), w_in, ln_g.reshape(1, d_a), ln_b.reshape(1, d_a), ws2, bias2, w_out)


def _mixer_b_kernel(x_ref, g_ref, win_ref, cw_ref, wout_ref, h1_ref, h2_ref,
                    xo_ref, zlast_ref, zs_ref, zext_ref, *, n_prompt_blocks, blocks_per_seq,
                    sample_len):
    i = pl.program_id(0)
    tb = x_ref.shape[0]
    d_b = wout_ref.shape[0]
    halo = SUBLANES
    x = x_ref[...]
    h = _rms(x, g_ref[...]).astype(BF16)
    bcz = jnp.dot(h, win_ref[...], preferred_element_type=F32)
    b_gate = bcz[:, :d_b]
    z = bcz[:, d_b:2 * d_b] * bcz[:, 2 * d_b:]
    is_sample = i >= n_prompt_blocks

    @pl.when(jnp.logical_or(is_sample, i % blocks_per_seq == 0))
    def _():
        zext_ref[0:halo, :] = jnp.zeros((halo, d_b), F32)

    zext_ref[halo:, :] = z
    z1 = zext_ref[halo - 1:halo - 1 + tb, :]
    z2 = zext_ref[halo - 2:halo - 2 + tb, :]
    pos = lax.broadcasted_iota(jnp.int32, (tb, 1), 0) % sample_len
    z1 = jnp.where(jnp.logical_and(is_sample, pos < 1), h1_ref[...], z1)
    z2 = jnp.where(jnp.logical_and(is_sample, pos < 2), h2_ref[...], z2)
    conv = cw_ref[0:1, :] * z2 + cw_ref[1:2, :] * z1 + cw_ref[2:3, :] * z
    y = jnp.dot((b_gate * conv).astype(BF16), wout_ref[...], preferred_element_type=F32)
    xo_ref[...] = x + y
    tail = z[tb - halo:, :]
    zlast_ref[0] = tail
    zs_ref[...] = z
    zext_ref[0:halo, :] = tail


def _mixer_b(x, g, w_in, conv_w, w_out, h1, h2, n_prompt_blocks, blocks_per_seq, sample_len):
    t, d = x.shape
    tb = TOKEN_BLOCK
    d_b = w_out.shape[0]
    nb = t // tb
    n_sample_blocks = nb - n_prompt_blocks
    const = lambda i: (0, 0)
    sample_blk = lambda i: (jnp.maximum(i - n_prompt_blocks, 0), 0)
    kern = functools.partial(_mixer_b_kernel, n_prompt_blocks=n_prompt_blocks,
                             blocks_per_seq=blocks_per_seq, sample_len=sample_len)
    return pl.pallas_call(
        kern,
        grid=(nb,),
        in_specs=[
            pl.BlockSpec((tb, d), lambda i: (i, 0)),
            pl.BlockSpec((1, d), const),
            pl.BlockSpec((d, 3 * d_b), const),
            pl.BlockSpec((conv_w.shape[0], d_b), const),
            pl.BlockSpec((d_b, d), const),
            pl.BlockSpec((tb, d_b), sample_blk),
            pl.BlockSpec((tb, d_b), sample_blk),
        ],
        out_specs=[
            pl.BlockSpec((tb, d), lambda i: (i, 0)),
            pl.BlockSpec((1, SUBLANES, d_b), lambda i: (i, 0, 0)),
            pl.BlockSpec((tb, d_b), sample_blk),
        ],
        out_shape=[
            jax.ShapeDtypeStruct((t, d), F32),
            jax.ShapeDtypeStruct((nb, SUBLANES, d_b), F32),
            jax.ShapeDtypeStruct((n_sample_blocks * tb, d_b), F32),
        ],
        scratch_shapes=[pltpu.VMEM((tb + SUBLANES, d_b), F32)],
        compiler_params=_params(1),
        name="mixer_b",
    )(x, g.reshape(1, d), w_in, conv_w, w_out, h1, h2)


def _merge_exchange_pairs(n):
    pairs = []
    p = 1
    while p < n:
        k = p
        while k >= 1:
            for j in range(k % p, n - k, 2 * k):
                for i in range(min(k, n - j - k)):
                    if (i + j) // (2 * p) == (i + j + k) // (2 * p):
                        pairs.append((i + j, i + j + k))
            k //= 2
        p *= 2
    return pairs


def _top_sorted(v):
    n = len(v)
    v = list(v)
    for i, j in _merge_exchange_pairs(n):
        v[i], v[j] = jnp.maximum(v[i], v[j]), jnp.minimum(v[i], v[j])
    shift = SUBLANES // 2
    while shift >= 1:
        v = [jnp.maximum(v[i], pltpu.roll(v[n - 1 - i], shift, 0)) for i in range(n)]
        stride = n // 2
        while stride >= 1:
            for i in range(n):
                if i & stride == 0:
                    v[i], v[i + stride] = (jnp.maximum(v[i], v[i + stride]),
                                           jnp.minimum(v[i], v[i + stride]))
            stride //= 2
        shift //= 2
    return v


def _top_values(s, n):
    out = []
    cur = s
    for _ in range(n):
        m = jnp.max(cur, axis=0, keepdims=True)
        out.append(m)
        cur = jnp.where(cur == m, NEG_INF, cur)
    return out


def _peer_kernel(x_ref, g_ref, wqt_ref, keys_ref, keyl_ref, u_ref, v_ref, xo_ref,
                 tokt_ref, qth_ref, qtl_ref, sc1_ref, th_ref, s2_ref, e2_ref, c1_ref,
                 at0_ref, at1_ref, mt0_ref, mt1_ref, acc_ref, *, n_items, groups_per_block):
    n = pl.program_id(0)
    nk = groups_per_block
    tb = x_ref.shape[0]
    n_heads = keys_ref.shape[0]
    n_keys = keys_ref.shape[2]
    d_half = keys_ref.shape[3]
    n_lane_tiles = tb // LANES
    key_tiles = n_keys // SUBLANES

    def transpose_tokens():
        tok = _rms(x_ref[...], g_ref[...])
        tokt_ref[...] = tok.T.astype(BF16)

    @pl.when(n == 0)
    def _init():
        for ref in (th_ref, s2_ref, e2_ref, c1_ref, at0_ref, at1_ref, mt0_ref, mt1_ref,
                    acc_ref):
            ref[...] = jnp.zeros(ref.shape, ref.dtype)
        transpose_tokens()

    @pl.when(jnp.logical_and(n >= 2, (n - 2) % nk == 0))
    def _start_block():
        acc_ref[...] = x_ref[...]

    d_model = v_ref.shape[1]
    gate_tiles = n_lane_tiles * (n_keys // GATE_ROWS)
    mxu_n = 2 * LANES
    col_chunks = tb // mxu_n
    a_chunks = A_ROW_CHUNKS * col_chunks
    tok_chunks = tb // C_TOKEN_CHUNK
    c_chunks = tok_chunks * (d_model // mxu_n)
    a_every = gate_tiles // a_chunks
    c_every = gate_tiles // c_chunks

    def stage_a(at_a, chunk):
        n_rows = (KEYS_PER_STEP * n_keys) // A_ROW_CHUNKS
        rows = slice((chunk // col_chunks) * n_rows, (chunk // col_chunks + 1) * n_rows)
        cols = slice((chunk % col_chunks) * mxu_n, (chunk % col_chunks + 1) * mxu_n)
        at_a[rows, cols] = jnp.dot(u_ref[rows, :], tokt_ref[:, cols], preferred_element_type=F32)

    def stage_c(mt_c, chunk):
        toks = slice((chunk % tok_chunks) * C_TOKEN_CHUNK, (chunk % tok_chunks + 1) * C_TOKEN_CHUNK)
        cols = slice((chunk // tok_chunks) * mxu_n, (chunk // tok_chunks + 1) * mxu_n)
        acc_ref[toks, cols] += lax.dot_general(mt_c[:, toks], v_ref[:, cols], (((0,), (0,)), ((), ())),
                                               preferred_element_type=F32)

    def stages(at_a, at_b, mt_b, mt_c):
        kb = (n + nk - 1) % nk
        tile = 0
        for j in range(n_lane_tiles):
            lanes = slice(j * LANES, (j + 1) * LANES)
            for r in range(n_keys // GATE_ROWS):
                if tile % a_every == 0:
                    stage_a(at_a, tile // a_every)
                if tile % c_every == min(a_every, c_every) // 2:
                    stage_c(mt_c, tile // c_every)
                tile += 1
                rows = slice(r * GATE_ROWS, (r + 1) * GATE_ROWS)
                for ii0 in range(0, KEYS_PER_STEP, GATE_KEYS):
                    w = [jnp.zeros((GATE_ROWS, LANES), F32)] * GATE_KEYS
                    for h in range(n_heads):
                        s2t = s2_ref[h, rows, lanes]
                        e2t = e2_ref[h, rows, lanes]
                        for ii in range(GATE_KEYS):
                            throw = th_ref[h, kb, ii0 + ii:ii0 + ii + 1, lanes]
                            c1row = c1_ref[h, kb, ii0 + ii:ii0 + ii + 1, lanes]
                            w[ii] = w[ii] + jnp.where(s2t >= throw, e2t, 0.0) * c1row
                    for ii in range(GATE_KEYS):
                        row0 = (ii0 + ii) * n_keys + r * GATE_ROWS
                        arow = slice(row0, row0 + GATE_ROWS)
                        a = at_b[arow, lanes]
                        aw = a * w[ii]
                        t = jnp.tanh(a * (GELU_K0 + GELU_K1 * (a * a)))
                        mt_b[arow, lanes] = (aw + aw * t).astype(BF16)

    @pl.when(n % 2 == 0)
    def _even():
        stages(at0_ref, at1_ref, mt1_ref, mt0_ref)

    @pl.when(n % 2 == 1)
    def _odd():
        stages(at1_ref, at0_ref, mt0_ref, mt1_ref)

    @pl.when(jnp.logical_and(n >= 2, (n - 2) % nk == nk - 1))
    def _finish_block():
        xo_ref[...] = acc_ref[...]

    @pl.when(jnp.logical_and((n + 1) % nk == 0, n + 1 < n_items))
    def _next_tokens():
        transpose_tokens()

    @pl.when(jnp.logical_and(n % nk == 0, n < n_items))
    def _route():
        qt = jnp.dot(wqt_ref[...], tokt_ref[...], preferred_element_type=F32)
        qt_hi = qt.astype(BF16)
        qth_ref[...] = qt_hi
        qtl_ref[...] = (qt - qt_hi.astype(F32)).astype(BF16)

        def head(h, carry):
            for p in range(2):
                q_rows = pl.ds(pl.multiple_of((2 * h + p) * d_half, d_half), d_half)
                q_hi = qth_ref[q_rows, :]
                s = (jnp.dot(keys_ref[h, p], q_hi, preferred_element_type=F32)
                     + jnp.dot(keys_ref[h, p], qtl_ref[q_rows, :], preferred_element_type=F32)
                     + jnp.dot(keyl_ref[h, p], q_hi, preferred_element_type=F32))
                if p == 0:
                    sc1_ref[...] = s.reshape(key_tiles, SUBLANES, tb)
                else:
                    s2_ref[h] = s
            sublane = lax.broadcasted_iota(jnp.int32, (SUBLANES, LANES), 0)

            def pack(rows):
                out = rows[-1]
                for i in range(len(rows) - 2, -1, -1):
                    out = jnp.where(sublane == i, rows[i], out)
                return out

            for j in range(n_lane_tiles):
                lanes = slice(j * LANES, (j + 1) * LANES)
                s1 = sc1_ref[:, :, lanes]
                s2 = s2_ref[h, :, lanes].reshape(key_tiles, SUBLANES, LANES)
                r1 = _top_sorted([s1[i] for i in range(key_tiles)])
                r2 = _top_sorted([s2[i] for i in range(key_tiles)])
                half = TOPK_HALF // 2
                sv2_lo, sv2_hi, sv1_hi = pack(r2[:half]), pack(r2[half:]), pack(r1[half:])
                cand = ([r1[0] + sv2_lo, r1[0] + sv2_hi] + [r1[a] + sv2_lo for a in range(1, half)]
                        + [sv1_hi + r2[0]])
                best = _top_values(jnp.concatenate(cand, axis=0), TOPK)
                zsum = jnp.zeros_like(best[0])
                for m in best:
                    zsum = zsum + jnp.exp(m - best[0])
                tau = jnp.broadcast_to(best[-1], (SUBLANES, LANES))
                th = jnp.full(s1.shape, jnp.inf, F32)
                th_top = jnp.full((SUBLANES, LANES), jnp.inf, F32)
                for b in range(TOPK_HALF):
                    if b < half:
                        th = jnp.where(s1 + r2[b][None] >= tau[None], r2[b][None], th)
                    th_top = jnp.where(r1[0] + r2[b] >= tau, r2[b], th_top)
                th_ref[h, :, :, lanes] = jnp.where(s1 == r1[0][None], th_top[None], th)
                scale = jnp.broadcast_to(0.5 / zsum, (SUBLANES, LANES))
                c1_ref[h, :, :, lanes] = jnp.exp(s1 - r1[0][None]) * scale[None]
                e2_ref[h, :, lanes] = jnp.exp(s2 - r2[0][None]).reshape(n_keys, LANES)
            return carry

        lax.fori_loop(0, n_heads, head, 0)


def _peer(x, g, wq_t, keys, u_tab, v_tab):
    t, d = x.shape
    tb = TOKEN_BLOCK
    n_heads, _, n_keys, d_half = keys.shape
    n_experts = u_tab.shape[0]
    eb = KEYS_PER_STEP * n_keys
    nk = n_experts // eb
    n_blocks = t // tb
    n_items = n_blocks * nk
    assert nk >= 4 and n_keys == TOPK_HALF * SUBLANES
    keys_hi = keys.astype(BF16)
    keys_lo = (keys - keys_hi.astype(F32)).astype(BF16)
    kern = functools.partial(_peer_kernel, n_items=n_items, groups_per_block=nk)
    key_tiles = n_keys // SUBLANES
    return pl.pallas_call(
        kern,
        grid=(n_items + 2,),
        in_specs=[
            pl.BlockSpec((tb, d), lambda n: (jnp.minimum((n + 1) // nk, n_blocks - 1), 0)),
            pl.BlockSpec((1, d), lambda n: (0, 0)),
            pl.BlockSpec(wq_t.shape, lambda n: (0, 0)),
            pl.BlockSpec(keys.shape, lambda n: (0, 0, 0, 0)),
            pl.BlockSpec(keys.shape, lambda n: (0, 0, 0, 0)),
            pl.BlockSpec((eb, d), lambda n: (jnp.minimum(n, n_items - 1) % nk, 0)),
            pl.BlockSpec((eb, d), lambda n: (jnp.maximum(n - 2, 0) % nk, 0)),
        ],
        out_specs=pl.BlockSpec((tb, d), lambda n: (jnp.maximum(n - 2, 0) // nk, 0)),
        out_shape=jax.ShapeDtypeStruct((t, d), F32),
        scratch_shapes=[
            pltpu.VMEM((d, tb), BF16),
            pltpu.VMEM((wq_t.shape[0], tb), BF16),
            pltpu.VMEM((wq_t.shape[0], tb), BF16),
            pltpu.VMEM((key_tiles, SUBLANES, tb), F32),
            pltpu.VMEM((n_heads, key_tiles, SUBLANES, tb), F32),
            pltpu.VMEM((n_heads, n_keys, tb), F32),
            pltpu.VMEM((n_heads, n_keys, tb), F32),
            pltpu.VMEM((n_heads, key_tiles, SUBLANES, tb), F32),
            pltpu.VMEM((eb, tb), F32),
            pltpu.VMEM((eb, tb), F32),
            pltpu.VMEM((eb, tb), BF16),
            pltpu.VMEM((eb, tb), BF16),
            pltpu.VMEM((tb, d), F32),
        ],
        compiler_params=_params(1),
        name="peer",
    )(x, g.reshape(1, d), wq_t, keys_hi, keys_lo, u_tab, v_tab)


def _final_norm_kernel(x_ref, g_ref, o_ref):
    o_ref[...] = _rms(x_ref[...], g_ref[...])


def _final_norm(x, g):
    t, d = x.shape
    tb = TOKEN_BLOCK
    return pl.pallas_call(
        _final_norm_kernel,
        grid=(t // tb,),
        in_specs=[pl.BlockSpec((tb, d), lambda i: (i, 0)), pl.BlockSpec((1, d), lambda i: (0, 0))],
        out_specs=pl.BlockSpec((tb, d), lambda i: (i, 0)),
        out_shape=jax.ShapeDtypeStruct((t, d), F32),
        compiler_params=_params(1),
        name="final_norm",
    )(x, g.reshape(1, d))


def _spatial_weights(w_s, b_s, sample_len, d_a):
    n_g = w_s.shape[0]
    d_g = d_a // n_g
    tril = lambda n: jnp.tril(jnp.ones((n, n), dtype=bool))
    ws_p = jnp.where(tril(CHUNK)[None], w_s, 0)
    ws4 = jnp.where(tril(sample_len)[None], w_s[:, :sample_len, :sample_len], 0)
    reps = CHUNK // sample_len
    eye = jnp.eye(reps, dtype=w_s.dtype)
    ws_s = (eye[None, :, None, :, None] * ws4[:, None, :, None, :]).reshape(n_g, CHUNK, CHUNK)
    ws2 = jnp.stack([ws_p, ws_s]).astype(BF16)
    bias_p = jnp.repeat(jnp.transpose(b_s), d_g, axis=1)
    bias_s = jnp.repeat(jnp.tile(jnp.transpose(b_s[:, :sample_len]), (reps, 1)), d_g, axis=1)
    return ws2, jnp.stack([bias_p, bias_s]).astype(F32)


def kernel(x_prompt, x_sample, state_conv, norm_mix, norm_ffn, norm_final, a_w_in, a_ln_g, a_ln_b,
           a_w_s, a_b_s, a_w_out, b_w_in, b_conv_w, b_w_out, peer_w_query, peer_sub_keys,
           peer_expert_u, peer_expert_v):
    batch, seq, d = x_prompt.shape
    n_seq_s, sample_len, _ = x_sample.shape
    depth = norm_mix.shape[0]
    d_a = a_ln_g.shape[1]
    d_b = b_w_out.shape[1]
    conv_taps = b_conv_w.shape[1]
    tb = TOKEN_BLOCK
    n_prompt = batch * seq
    n_sample = n_seq_s * sample_len
    assert seq % tb == 0 and n_sample % tb == 0 and tb % CHUNK == 0
    assert CHUNK % sample_len == 0 and sample_len >= conv_taps - 1 and conv_taps == 3
    assert peer_sub_keys.shape[3] % (8 * SUBLANES) == 0
    n_prompt_blocks = n_prompt // tb

    x = jnp.concatenate([x_prompt.reshape(n_prompt, d), x_sample.reshape(n_sample, d)], axis=0)
    v_rows, conv_prompt, conv_sample = [], [], []
    for i in range(depth):
        j = i // 2
        if i % 2 == 0:
            ws2, bias2 = _spatial_weights(a_w_s[j], a_b_s[j], sample_len, d_a)
            x, v = _mixer_a(x, norm_mix[i], a_w_in[j].astype(BF16), a_ln_g[j], a_ln_b[j], ws2, bias2,
                            a_w_out[j].astype(BF16), n_prompt_blocks)
            v_rows.append(v.reshape(n_seq_s, sample_len, d_a))
        else:
            buf = state_conv[j]
            pad = lambda rows: jnp.concatenate(
                [rows, jnp.zeros((n_seq_s, sample_len - rows.shape[1], d_b), F32)], axis=1
            ).reshape(n_sample, d_b)
            h1 = pad(buf[:, 1:2])
            h2 = pad(buf[:, 0:2])
            x, zlast, zs = _mixer_b(x, norm_mix[i], b_w_in[j].astype(BF16), b_conv_w[j],
                                    b_w_out[j].astype(BF16), h1, h2, n_prompt_blocks, seq // tb,
                                    sample_len)
            blocks_per_seq = seq // tb
            conv_prompt.append(zlast[blocks_per_seq - 1:n_prompt_blocks:blocks_per_seq, SUBLANES - 2:])
            conv_sample.append(zs.reshape(n_seq_s, sample_len, d_b)[:, sample_len - 2:])
        x = _peer(x, norm_ffn[i], jnp.transpose(peer_w_query[i]).astype(BF16), peer_sub_keys[i],
                  peer_expert_u[i].astype(BF16), peer_expert_v[i].astype(BF16))
    y = _final_norm(x, norm_final)
    return (y[:n_prompt].reshape(batch, seq, d), y[n_prompt:].reshape(n_seq_s, sample_len, d),
            jnp.stack(v_rows), jnp.stack(conv_prompt), jnp.stack(conv_sample))
```

```python
import functools

import jax
import jax.numpy as jnp
from jax import lax
from jax.experimental import pallas as pl
from jax.experimental.pallas import tpu as pltpu

F32 = jnp.float32
BF16 = jnp.bfloat16

EPS = 1e-6
CHUNK = 128
N_GROUPS_A = 8
TOPK_HALF = 16
TOPK = 16
LANES = 128
SUBLANES = 8
TOKEN_BLOCK = 512
KEYS_PER_STEP = SUBLANES
GATE_ROWS = 2 * SUBLANES
GATE_KEYS = KEYS_PER_STEP
A_ROW_CHUNKS = 8
C_TOKEN_CHUNK = 512
VMEM_LIMIT = 56 * 1024 * 1024
GELU_K0 = 0.7978845608028654
GELU_K1 = GELU_K0 * 0.044715
NEG_INF = float("-inf")


def _rms(x, g):
    return x * lax.rsqrt(jnp.mean(x * x, axis=-1, keepdims=True) + EPS) * g


def _params(n_axes):
    return pltpu.CompilerParams(dimension_semantics=("arbitrary",) * n_axes,
                                vmem_limit_bytes=VMEM_LIMIT)


def _mixer_a_kernel(x_ref, g_ref, win_ref, lng_ref, lnb_ref, ws_ref, bias_ref, wout_ref,
                    xo_ref, v_ref, gated_ref):
    tb = x_ref.shape[0]
    d_a = lng_ref.shape[1]
    d_g = d_a // N_GROUPS_A
    x = x_ref[...]
    h = _rms(x, g_ref[...]).astype(BF16)
    z = jax.nn.gelu(jnp.dot(h, win_ref[...], preferred_element_type=F32))
    v = z[:, d_a:]
    mu = jnp.mean(v, axis=-1, keepdims=True)
    vc = v - mu
    var = jnp.mean(vc * vc, axis=-1, keepdims=True)
    vn = vc * lax.rsqrt(var + EPS) * lng_ref[...] + lnb_ref[...]
    v_ref[...] = vn
    vb = vn.astype(BF16)
    for c in range(tb // CHUNK):
        rows = slice(c * CHUNK, (c + 1) * CHUNK)
        for g in range(N_GROUPS_A):
            cols = slice(g * d_g, (g + 1) * d_g)
            mixed = jnp.dot(ws_ref[0, g], vb[rows, cols], preferred_element_type=F32)
            mixed = mixed + bias_ref[0, :, cols]
            gated_ref[rows, cols] = (z[rows, cols] * mixed).astype(BF16)
    y = jnp.dot(gated_ref[...], wout_ref[...], preferred_element_type=F32)
    xo_ref[...] = x + y


def _mixer_a(x, g, w_in, ln_g, ln_b, ws2, bias2, w_out, n_prompt_blocks):
    t, d = x.shape
    tb = TOKEN_BLOCK
    d_a = ln_g.shape[0]
    nb = t // tb
    n_sample_blocks = nb - n_prompt_blocks
    is_sample = lambda i: jnp.where(i >= n_prompt_blocks, 1, 0)
    const = lambda i: (0, 0)
    return pl.pallas_call(
        _mixer_a_kernel,
        grid=(nb,),
        in_specs=[
            pl.BlockSpec((tb, d), lambda i: (i, 0)),
            pl.BlockSpec((1, d), const),
            pl.BlockSpec((d, 2 * d_a), const),
            pl.BlockSpec((1, d_a), const),
            pl.BlockSpec((1, d_a), const),
            pl.BlockSpec((1, N_GROUPS_A, CHUNK, CHUNK), lambda i: (is_sample(i), 0, 0, 0)),
            pl.BlockSpec((1, CHUNK, d_a), lambda i: (is_sample(i), 0, 0)),
            pl.BlockSpec((d_a, d), const),
        ],
        out_specs=[
            pl.BlockSpec((tb, d), lambda i: (i, 0)),
            pl.BlockSpec((tb, d_a), lambda i: (jnp.maximum(i - n_prompt_blocks, 0), 0)),
        ],
        out_shape=[
            jax.ShapeDtypeStruct((t, d), F32),
            jax.ShapeDtypeStruct((n_sample_blocks * tb, d_a), F32),
        ],
        scratch_shapes=[pltpu.VMEM((tb, d_a), BF16)],
        compiler_params=_params(1),
        name="mixer_a",
    )(x, g.reshape(1, d), w_in, ln_g.reshape(1, d_a), ln_b.reshape(1, d_a), ws2, bias2, w_out)


def _mixer_b_kernel(x_ref, g_ref, win_ref, cw_ref, wout_ref, h1_ref, h2_ref,
                    xo_ref, zlast_ref, zs_ref, zext_ref, *, n_prompt_blocks, blocks_per_seq,
                    sample_len):
    i = pl.program_id(0)
    tb = x_ref.shape[0]
    d_b = wout_ref.shape[0]
    halo = SUBLANES
    x = x_ref[...]
    h = _rms(x, g_ref[...]).astype(BF16)
    bcz = jnp.dot(h, win_ref[...], preferred_element_type=F32)
    b_gate = bcz[:, :d_b]
    z = bcz[:, d_b:2 * d_b] * bcz[:, 2 * d_b:]
    is_sample = i >= n_prompt_blocks

    @pl.when(jnp.logical_or(is_sample, i % blocks_per_seq == 0))
    def _():
        zext_ref[0:halo, :] = jnp.zeros((halo, d_b), F32)

    zext_ref[halo:, :] = z
    z1 = zext_ref[halo - 1:halo - 1 + tb, :]
    z2 = zext_ref[halo - 2:halo - 2 + tb, :]
    pos = lax.broadcasted_iota(jnp.int32, (tb, 1), 0) % sample_len
    z1 = jnp.where(jnp.logical_and(is_sample, pos < 1), h1_ref[...], z1)
    z2 = jnp.where(jnp.logical_and(is_sample, pos < 2), h2_ref[...], z2)
    conv = cw_ref[0:1, :] * z2 + cw_ref[1:2, :] * z1 + cw_ref[2:3, :] * z
    y = jnp.dot((b_gate * conv).astype(BF16), wout_ref[...], preferred_element_type=F32)
    xo_ref[...] = x + y
    tail = z[tb - halo:, :]
    zlast_ref[0] = tail
    zs_ref[...] = z
    zext_ref[0:halo, :] = tail


def _mixer_b(x, g, w_in, conv_w, w_out, h1, h2, n_prompt_blocks, blocks_per_seq, sample_len):
    t, d = x.shape
    tb = TOKEN_BLOCK
    d_b = w_out.shape[0]
    nb = t // tb
    n_sample_blocks = nb - n_prompt_blocks
    const = lambda i: (0, 0)
    sample_blk = lambda i: (jnp.maximum(i - n_prompt_blocks, 0), 0)
    kern = functools.partial(_mixer_b_kernel, n_prompt_blocks=n_prompt_blocks,
                             blocks_per_seq=blocks_per_seq, sample_len=sample_len)
    return pl.pallas_call(
        kern,
        grid=(nb,),
        in_specs=[
            pl.BlockSpec((tb, d), lambda i: (i, 0)),
            pl.BlockSpec((1, d), const),
            pl.BlockSpec((d, 3 * d_b), const),
            pl.BlockSpec((conv_w.shape[0], d_b), const),
            pl.BlockSpec((d_b, d), const),
            pl.BlockSpec((tb, d_b), sample_blk),
            pl.BlockSpec((tb, d_b), sample_blk),
        ],
        out_specs=[
            pl.BlockSpec((tb, d), lambda i: (i, 0)),
            pl.BlockSpec((1, SUBLANES, d_b), lambda i: (i, 0, 0)),
            pl.BlockSpec((tb, d_b), sample_blk),
        ],
        out_shape=[
            jax.ShapeDtypeStruct((t, d), F32),
            jax.ShapeDtypeStruct((nb, SUBLANES, d_b), F32),
            jax.ShapeDtypeStruct((n_sample_blocks * tb, d_b), F32),
        ],
        scratch_shapes=[pltpu.VMEM((tb + SUBLANES, d_b), F32)],
        compiler_params=_params(1),
        name="mixer_b",
    )(x, g.reshape(1, d), w_in, conv_w, w_out, h1, h2)


def _merge_exchange_pairs(n):
    pairs = []
    p = 1
    while p < n:
        k = p
        while k >= 1:
            for j in range(k % p, n - k, 2 * k):
                for i in range(min(k, n - j - k)):
                    if (i + j) // (2 * p) == (i + j + k) // (2 * p):
                        pairs.append((i + j, i + j + k))
            k //= 2
        p *= 2
    return pairs


def _top_sorted(v):
    n = len(v)
    v = list(v)
    for i, j in _merge_exchange_pairs(n):
        v[i], v[j] = jnp.maximum(v[i], v[j]), jnp.minimum(v[i], v[j])
    shift = SUBLANES // 2
    while shift >= 1:
        v = [jnp.maximum(v[i], pltpu.roll(v[n - 1 - i], shift, 0)) for i in range(n)]
        stride = n // 2
        while stride >= 1:
            for i in range(n):
                if i & stride == 0:
                    v[i], v[i + stride] = (jnp.maximum(v[i], v[i + stride]),
                                           jnp.minimum(v[i], v[i + stride]))
            stride //= 2
        shift //= 2
    return v


def _top_values(s, n):
    out = []
    cur = s
    for _ in range(n):
        m = jnp.max(cur, axis=0, keepdims=True)
        out.append(m)
        cur = jnp.where(cur == m, NEG_INF, cur)
    return out


def _peer_kernel(x_ref, g_ref, wqt_ref, keys_ref, keyl_ref, u_ref, v_ref, xo_ref,
                 tokt_ref, qth_ref, qtl_ref, sc1_ref, th_ref, s2_ref, e2_ref, c1_ref,
                 at0_ref, at1_ref, mt0_ref, mt1_ref, acc_ref, *, n_items, groups_per_block):
    n = pl.program_id(0)
    nk = groups_per_block
    tb = x_ref.shape[0]
    n_heads = keys_ref.shape[0]
    n_keys = keys_ref.shape[2]
    d_half = keys_ref.shape[3]
    n_lane_tiles = tb // LANES
    key_tiles = n_keys // SUBLANES

    def transpose_tokens():
        tok = _rms(x_ref[...], g_ref[...])
        tokt_ref[...] = tok.T.astype(BF16)

    @pl.when(n == 0)
    def _init():
        for ref in (th_ref, s2_ref, e2_ref, c1_ref, at0_ref, at1_ref, mt0_ref, mt1_ref,
                    acc_ref):
            ref[...] = jnp.zeros(ref.shape, ref.dtype)
        transpose_tokens()

    @pl.when(jnp.logical_and(n >= 2, (n - 2) % nk == 0))
    def _start_block():
        acc_ref[...] = x_ref[...]

    d_model = v_ref.shape[1]
    gate_tiles = n_lane_tiles * (n_keys // GATE_ROWS)
    mxu_n = 2 * LANES
    col_chunks = tb // mxu_n
    a_chunks = A_ROW_CHUNKS * col_chunks
    tok_chunks = tb // C_TOKEN_CHUNK
    c_chunks = tok_chunks * (d_model // mxu_n)
    a_every = gate_tiles // a_chunks
    c_every = gate_tiles // c_chunks

    def stage_a(at_a, chunk):
        n_rows = (KEYS_PER_STEP * n_keys) // A_ROW_CHUNKS
        rows = slice((chunk // col_chunks) * n_rows, (chunk // col_chunks + 1) * n_rows)
        cols = slice((chunk % col_chunks) * mxu_n, (chunk % col_chunks + 1) * mxu_n)
        at_a[rows, cols] = jnp.dot(u_ref[rows, :], tokt_ref[:, cols], preferred_element_type=F32)

    def stage_c(mt_c, chunk):
        toks = slice((chunk % tok_chunks) * C_TOKEN_CHUNK, (chunk % tok_chunks + 1) * C_TOKEN_CHUNK)
        cols = slice((chunk // tok_chunks) * mxu_n, (chunk // tok_chunks + 1) * mxu_n)
        acc_ref[toks, cols] += lax.dot_general(mt_c[:, toks], v_ref[:, cols], (((0,), (0,)), ((), ())),
                                               preferred_element_type=F32)

    def stages(at_a, at_b, mt_b, mt_c):
        kb = (n + nk - 1) % nk
        tile = 0
        for j in range(n_lane_tiles):
            lanes = slice(j * LANES, (j + 1) * LANES)
            for r in range(n_keys // GATE_ROWS):
                if tile % a_every == 0:
                    stage_a(at_a, tile // a_every)
                if tile % c_every == min(a_every, c_every) // 2:
                    stage_c(mt_c, tile // c_every)
                tile += 1
                rows = slice(r * GATE_ROWS, (r + 1) * GATE_ROWS)
                for ii0 in range(0, KEYS_PER_STEP, GATE_KEYS):
                    w = [jnp.zeros((GATE_ROWS, LANES), F32)] * GATE_KEYS
                    for h in range(n_heads):
                        s2t = s2_ref[h, rows, lanes]
                        e2t = e2_ref[h, rows, lanes]
                        for ii in range(GATE_KEYS):
                            throw = th_ref[h, kb, ii0 + ii:ii0 + ii + 1, lanes]
                            c1row = c1_ref[h, kb, ii0 + ii:ii0 + ii + 1, lanes]
                            w[ii] = w[ii] + jnp.where(s2t >= throw, e2t, 0.0) * c1row
                    for ii in range(GATE_KEYS):
                        row0 = (ii0 + ii) * n_keys + r * GATE_ROWS
                        arow = slice(row0, row0 + GATE_ROWS)
                        a = at_b[arow, lanes].astype(BF16)
                        aw = a * w[ii].astype(BF16)
                        t = jnp.tanh(a * (GELU_K0 + GELU_K1 * (a * a)))
                        mt_b[arow, lanes] = aw + aw * t

    @pl.when(n % 2 == 0)
    def _even():
        stages(at0_ref, at1_ref, mt1_ref, mt0_ref)

    @pl.when(n % 2 == 1)
    def _odd():
        stages(at1_ref, at0_ref, mt0_ref, mt1_ref)

    @pl.when(jnp.logical_and(n >= 2, (n - 2) % nk == nk - 1))
    def _finish_block():
        xo_ref[...] = acc_ref[...]

    @pl.when(jnp.logical_and((n + 1) % nk == 0, n + 1 < n_items))
    def _next_tokens():
        transpose_tokens()

    @pl.when(jnp.logical_and(n % nk == 0, n < n_items))
    def _route():
        qt = jnp.dot(wqt_ref[...], tokt_ref[...], preferred_element_type=F32)
        qt_hi = qt.astype(BF16)
        qth_ref[...] = qt_hi
        qtl_ref[...] = (qt - qt_hi.astype(F32)).astype(BF16)

        def head(h, carry):
            for p in range(2):
                q_rows = pl.ds(pl.multiple_of((2 * h + p) * d_half, d_half), d_half)
                q_hi = qth_ref[q_rows, :]
                s = (jnp.dot(keys_ref[h, p], q_hi, preferred_element_type=F32)
                     + jnp.dot(keys_ref[h, p], qtl_ref[q_rows, :], preferred_element_type=F32)
                     + jnp.dot(keyl_ref[h, p], q_hi, preferred_element_type=F32))
                if p == 0:
                    sc1_ref[...] = s.reshape(key_tiles, SUBLANES, tb)
                else:
                    s2_ref[h] = s
            sublane = lax.broadcasted_iota(jnp.int32, (SUBLANES, LANES), 0)

            def pack(rows):
                out = rows[-1]
                for i in range(len(rows) - 2, -1, -1):
                    out = jnp.where(sublane == i, rows[i], out)
                return out

            for j in range(n_lane_tiles):
                lanes = slice(j * LANES, (j + 1) * LANES)
                s1 = sc1_ref[:, :, lanes]
                s2 = s2_ref[h, :, lanes].reshape(key_tiles, SUBLANES, LANES)
                r1 = _top_sorted([s1[i] for i in range(key_tiles)])
                r2 = _top_sorted([s2[i] for i in range(key_tiles)])
                half = TOPK_HALF // 2
                sv2_lo, sv2_hi, sv1_hi = pack(r2[:half]), pack(r2[half:]), pack(r1[half:])
                cand = ([r1[0] + sv2_lo, r1[0] + sv2_hi] + [r1[a] + sv2_lo for a in range(1, half)]
                        + [sv1_hi + r2[0]])
                best = _top_values(jnp.concatenate(cand, axis=0), TOPK)
                zsum = jnp.zeros_like(best[0])
                for m in best:
                    zsum = zsum + jnp.exp(m - best[0])
                tau = jnp.broadcast_to(best[-1], (SUBLANES, LANES))
                th = jnp.full(s1.shape, jnp.inf, F32)
                th_top = jnp.full((SUBLANES, LANES), jnp.inf, F32)
                for b in range(TOPK_HALF):
                    if b < half:
                        th = jnp.where(s1 + r2[b][None] >= tau[None], r2[b][None], th)
                    th_top = jnp.where(r1[0] + r2[b] >= tau, r2[b], th_top)
                th_ref[h, :, :, lanes] = jnp.where(s1 == r1[0][None], th_top[None], th)
                scale = jnp.broadcast_to(0.5 / zsum, (SUBLANES, LANES))
                c1_ref[h, :, :, lanes] = jnp.exp(s1 - r1[0][None]) * scale[None]
                e2_ref[h, :, lanes] = jnp.exp(s2 - r2[0][None]).reshape(n_keys, LANES)
            return carry

        lax.fori_loop(0, n_heads, head, 0)


def _peer(x, g, wq_t, keys, u_tabs, v_tabs, layer):
    t, d = x.shape
    tb = TOKEN_BLOCK
    n_heads, _, n_keys, d_half = keys.shape
    n_experts = u_tabs.shape[1]
    eb = KEYS_PER_STEP * n_keys
    nk = n_experts // eb
    n_blocks = t // tb
    n_items = n_blocks * nk
    assert nk >= 4 and n_keys == TOPK_HALF * SUBLANES
    keys_hi = keys.astype(BF16)
    keys_lo = (keys - keys_hi.astype(F32)).astype(BF16)
    kern = functools.partial(_peer_kernel, n_items=n_items, groups_per_block=nk)
    key_tiles = n_keys // SUBLANES
    return pl.pallas_call(
        kern,
        grid=(n_items + 2,),
        in_specs=[
            pl.BlockSpec((tb, d), lambda n: (jnp.minimum((n + 1) // nk, n_blocks - 1), 0)),
            pl.BlockSpec((1, d), lambda n: (0, 0)),
            pl.BlockSpec(wq_t.shape, lambda n: (0, 0)),
            pl.BlockSpec(keys.shape, lambda n: (0, 0, 0, 0)),
            pl.BlockSpec(keys.shape, lambda n: (0, 0, 0, 0)),
            pl.BlockSpec((None, eb, d), lambda n: (layer, jnp.minimum(n, n_items - 1) % nk, 0)),
            pl.BlockSpec((None, eb, d), lambda n: (layer, jnp.maximum(n - 2, 0) % nk, 0)),
        ],
        out_specs=pl.BlockSpec((tb, d), lambda n: (jnp.maximum(n - 2, 0) // nk, 0)),
        out_shape=jax.ShapeDtypeStruct((t, d), F32),
        scratch_shapes=[
            pltpu.VMEM((d, tb), BF16),
            pltpu.VMEM((wq_t.shape[0], tb), BF16),
            pltpu.VMEM((wq_t.shape[0], tb), BF16),
            pltpu.VMEM((key_tiles, SUBLANES, tb), F32),
            pltpu.VMEM((n_heads, key_tiles, SUBLANES, tb), F32),
            pltpu.VMEM((n_heads, n_keys, tb), F32),
            pltpu.VMEM((n_heads, n_keys, tb), F32),
            pltpu.VMEM((n_heads, key_tiles, SUBLANES, tb), F32),
            pltpu.VMEM((eb, tb), F32),
            pltpu.VMEM((eb, tb), F32),
            pltpu.VMEM((eb, tb), BF16),
            pltpu.VMEM((eb, tb), BF16),
            pltpu.VMEM((tb, d), F32),
        ],
        compiler_params=_params(1),
        name="peer",
    )(x, g.reshape(1, d), wq_t, keys_hi, keys_lo, u_tabs, v_tabs)


def _final_norm_kernel(x_ref, g_ref, yp_ref, ys_ref, *, n_prompt_blocks):
    i = pl.program_id(0)
    y = _rms(x_ref[...], g_ref[...])

    @pl.when(i < n_prompt_blocks)
    def _():
        yp_ref[...] = y

    @pl.when(i >= n_prompt_blocks)
    def _():
        ys_ref[...] = y


def _final_norm(x, g, n_prompt_blocks):
    t, d = x.shape
    tb = TOKEN_BLOCK
    nb = t // tb
    return pl.pallas_call(
        functools.partial(_final_norm_kernel, n_prompt_blocks=n_prompt_blocks),
        grid=(nb,),
        in_specs=[pl.BlockSpec((tb, d), lambda i: (i, 0)), pl.BlockSpec((1, d), lambda i: (0, 0))],
        out_specs=[
            pl.BlockSpec((tb, d), lambda i: (jnp.minimum(i, n_prompt_blocks - 1), 0)),
            pl.BlockSpec((tb, d), lambda i: (jnp.maximum(i - n_prompt_blocks, 0), 0)),
        ],
        out_shape=[
            jax.ShapeDtypeStruct((n_prompt_blocks * tb, d), F32),
            jax.ShapeDtypeStruct(((nb - n_prompt_blocks) * tb, d), F32),
        ],
        compiler_params=_params(1),
        name="final_norm",
    )(x, g.reshape(1, d))


def _spatial_weights(w_s, b_s, sample_len, d_a):
    n_g = w_s.shape[0]
    d_g = d_a // n_g
    tril = lambda n: jnp.tril(jnp.ones((n, n), dtype=bool))
    ws_p = jnp.where(tril(CHUNK)[None], w_s, 0)
    ws4 = jnp.where(tril(sample_len)[None], w_s[:, :sample_len, :sample_len], 0)
    reps = CHUNK // sample_len
    eye = jnp.eye(reps, dtype=w_s.dtype)
    ws_s = (eye[None, :, None, :, None] * ws4[:, None, :, None, :]).reshape(n_g, CHUNK, CHUNK)
    ws2 = jnp.stack([ws_p, ws_s]).astype(BF16)
    bias_p = jnp.repeat(jnp.transpose(b_s), d_g, axis=1)
    bias_s = jnp.repeat(jnp.tile(jnp.transpose(b_s[:, :sample_len]), (reps, 1)), d_g, axis=1)
    return ws2, jnp.stack([bias_p, bias_s]).astype(F32)


def kernel(x_prompt, x_sample, state_conv, norm_mix, norm_ffn, norm_final, a_w_in, a_ln_g, a_ln_b,
           a_w_s, a_b_s, a_w_out, b_w_in, b_conv_w, b_w_out, peer_w_query, peer_sub_keys,
           peer_expert_u, peer_expert_v):
    batch, seq, d = x_prompt.shape
    n_seq_s, sample_len, _ = x_sample.shape
    depth = norm_mix.shape[0]
    d_a = a_ln_g.shape[1]
    d_b = b_w_out.shape[1]
    conv_taps = b_conv_w.shape[1]
    tb = TOKEN_BLOCK
    n_prompt = batch * seq
    n_sample = n_seq_s * sample_len
    assert seq % tb == 0 and n_sample % tb == 0 and tb % CHUNK == 0
    assert CHUNK % sample_len == 0 and sample_len >= conv_taps - 1 and conv_taps == 3
    assert peer_sub_keys.shape[3] % (8 * SUBLANES) == 0
    n_prompt_blocks = n_prompt // tb

    x = jnp.concatenate([x_prompt.reshape(n_prompt, d), x_sample.reshape(n_sample, d)], axis=0)
    v_rows, conv_prompt, conv_sample = [], [], []
    expert_u = peer_expert_u.astype(BF16)
    expert_v = peer_expert_v.astype(BF16)
    for i in range(depth):
        j = i // 2
        if i % 2 == 0:
            ws2, bias2 = _spatial_weights(a_w_s[j], a_b_s[j], sample_len, d_a)
            x, v = _mixer_a(x, norm_mix[i], a_w_in[j].astype(BF16), a_ln_g[j], a_ln_b[j], ws2, bias2,
                            a_w_out[j].astype(BF16), n_prompt_blocks)
            v_rows.append(v.reshape(n_seq_s, sample_len, d_a))
        else:
            buf = state_conv[j]
            pad = lambda rows: jnp.concatenate(
                [rows, jnp.zeros((n_seq_s, sample_len - rows.shape[1], d_b), F32)], axis=1
            ).reshape(n_sample, d_b)
            h1 = pad(buf[:, 1:2])
            h2 = pad(buf[:, 0:2])
            x, zlast, zs = _mixer_b(x, norm_mix[i], b_w_in[j].astype(BF16), b_conv_w[j],
                                    b_w_out[j].astype(BF16), h1, h2, n_prompt_blocks, seq // tb,
                                    sample_len)
            blocks_per_seq = seq // tb
            conv_prompt.append(zlast[blocks_per_seq - 1:n_prompt_blocks:blocks_per_seq, SUBLANES - 2:])
            conv_sample.append(zs.reshape(n_seq_s, sample_len, d_b)[:, sample_len - 2:])
        x = _peer(x, norm_ffn[i], jnp.transpose(peer_w_query[i]).astype(BF16), peer_sub_keys[i],
                  expert_u, expert_v, i)
    y_prompt, y_sample = _final_norm(x, norm_final, n_prompt_blocks)
    return (y_prompt.reshape(batch, seq, d), y_sample.reshape(n_seq_s, sample_len, d),
            jnp.stack(v_rows), jnp.stack(conv_prompt), jnp.stack(conv_sample))
```

```python
import functools

import jax
import jax.numpy as jnp
from jax import lax
from jax.experimental import pallas as pl
from jax.experimental.pallas import tpu as pltpu

F32 = jnp.float32
BF16 = jnp.bfloat16

EPS = 1e-6
CHUNK = 128
N_GROUPS_A = 8
TOPK_HALF = 16
TOPK = 16
LANES = 128
SUBLANES = 8
TOKEN_BLOCK = 512
KEYS_PER_STEP = SUBLANES
GATE_ROWS = 2 * SUBLANES
GATE_KEYS = KEYS_PER_STEP
A_ROW_CHUNKS = 8
C_TOKEN_CHUNK = 512
C_COL_CHUNK = 256
VMEM_LIMIT = 60 * 1024 * 1024
GELU_K0 = 0.7978845608028654
GELU_K1 = GELU_K0 * 0.044715
NEG_INF = float("-inf")


def _rms(x, g):
    return x * lax.rsqrt(jnp.mean(x * x, axis=-1, keepdims=True) + EPS) * g


def _params(n_axes):
    return pltpu.CompilerParams(dimension_semantics=("arbitrary",) * n_axes,
                                vmem_limit_bytes=VMEM_LIMIT)


def _mixer_a_kernel(x_ref, g_ref, win_ref, lng_ref, lnb_ref, ws_ref, bias_ref, wout_ref,
                    xo_ref, v_ref, gated_ref):
    tb = x_ref.shape[0]
    d_a = lng_ref.shape[1]
    d_g = d_a // N_GROUPS_A
    x = x_ref[...]
    h = _rms(x, g_ref[...]).astype(BF16)
    z = jax.nn.gelu(jnp.dot(h, win_ref[...], preferred_element_type=F32))
    v = z[:, d_a:]
    mu = jnp.mean(v, axis=-1, keepdims=True)
    vc = v - mu
    var = jnp.mean(vc * vc, axis=-1, keepdims=True)
    vn = vc * lax.rsqrt(var + EPS) * lng_ref[...] + lnb_ref[...]
    v_ref[...] = vn
    vb = vn.astype(BF16)
    for c in range(tb // CHUNK):
        rows = slice(c * CHUNK, (c + 1) * CHUNK)
        for g in range(N_GROUPS_A):
            cols = slice(g * d_g, (g + 1) * d_g)
            mixed = jnp.dot(ws_ref[0, g], vb[rows, cols], preferred_element_type=F32)
            mixed = mixed + bias_ref[0, :, cols]
            gated_ref[rows, cols] = (z[rows, cols] * mixed).astype(BF16)
    y = jnp.dot(gated_ref[...], wout_ref[...], preferred_element_type=F32)
    xo_ref[...] = x + y


def _mixer_a(x, g, w_in, ln_g, ln_b, ws2, bias2, w_out, n_prompt_blocks):
    t, d = x.shape
    tb = TOKEN_BLOCK
    d_a = ln_g.shape[0]
    nb = t // tb
    n_sample_blocks = nb - n_prompt_blocks
    is_sample = lambda i: jnp.where(i >= n_prompt_blocks, 1, 0)
    const = lambda i: (0, 0)
    return pl.pallas_call(
        _mixer_a_kernel,
        grid=(nb,),
        in_specs=[
            pl.BlockSpec((tb, d), lambda i: (i, 0)),
            pl.BlockSpec((1, d), const),
            pl.BlockSpec((d, 2 * d_a), const),
            pl.BlockSpec((1, d_a), const),
            pl.BlockSpec((1, d_a), const),
            pl.BlockSpec((1, N_GROUPS_A, CHUNK, CHUNK), lambda i: (is_sample(i), 0, 0, 0)),
            pl.BlockSpec((1, CHUNK, d_a), lambda i: (is_sample(i), 0, 0)),
            pl.BlockSpec((d_a, d), const),
        ],
        out_specs=[
            pl.BlockSpec((tb, d), lambda i: (i, 0)),
            pl.BlockSpec((tb, d_a), lambda i: (jnp.maximum(i - n_prompt_blocks, 0), 0)),
        ],
        out_shape=[
            jax.ShapeDtypeStruct((t, d), F32),
            jax.ShapeDtypeStruct((n_sample_blocks * tb, d_a), F32),
        ],
        scratch_shapes=[pltpu.VMEM((tb, d_a), BF16)],
        compiler_params=_params(1),
        name="mixer_a",
    )(x, g.reshape(1, d), w_in, ln_g.reshape(1, d_a), ln_b.reshape(1, d_a), ws2, bias2, w_out)


def _mixer_b_kernel(x_ref, g_ref, win_ref, cw_ref, wout_ref, h1_ref, h2_ref,
                    xo_ref, zlast_ref, zs_ref, zext_ref, *, n_prompt_blocks, blocks_per_seq,
                    sample_len):
    i = pl.program_id(0)
    tb = x_ref.shape[0]
    d_b = wout_ref.shape[0]
    halo = SUBLANES
    x = x_ref[...]
    h = _rms(x, g_ref[...]).astype(BF16)
    bcz = jnp.dot(h, win_ref[...], preferred_element_type=F32)
    b_gate = bcz[:, :d_b]
    z = bcz[:, d_b:2 * d_b] * bcz[:, 2 * d_b:]
    is_sample = i >= n_prompt_blocks

    @pl.when(jnp.logical_or(is_sample, i % blocks_per_seq == 0))
    def _():
        zext_ref[0:halo, :] = jnp.zeros((halo, d_b), F32)

    zext_ref[halo:, :] = z
    z1 = zext_ref[halo - 1:halo - 1 + tb, :]
    z2 = zext_ref[halo - 2:halo - 2 + tb, :]
    pos = lax.broadcasted_iota(jnp.int32, (tb, 1), 0) % sample_len
    z1 = jnp.where(jnp.logical_and(is_sample, pos < 1), h1_ref[...], z1)
    z2 = jnp.where(jnp.logical_and(is_sample, pos < 2), h2_ref[...], z2)
    conv = cw_ref[0:1, :] * z2 + cw_ref[1:2, :] * z1 + cw_ref[2:3, :] * z
    y = jnp.dot((b_gate * conv).astype(BF16), wout_ref[...], preferred_element_type=F32)
    xo_ref[...] = x + y
    tail = z[tb - halo:, :]
    zlast_ref[0] = tail
    zs_ref[...] = z
    zext_ref[0:halo, :] = tail


def _mixer_b(x, g, w_in, conv_w, w_out, h1, h2, n_prompt_blocks, blocks_per_seq, sample_len):
    t, d = x.shape
    tb = TOKEN_BLOCK
    d_b = w_out.shape[0]
    nb = t // tb
    n_sample_blocks = nb - n_prompt_blocks
    const = lambda i: (0, 0)
    sample_blk = lambda i: (jnp.maximum(i - n_prompt_blocks, 0), 0)
    kern = functools.partial(_mixer_b_kernel, n_prompt_blocks=n_prompt_blocks,
                             blocks_per_seq=blocks_per_seq, sample_len=sample_len)
    return pl.pallas_call(
        kern,
        grid=(nb,),
        in_specs=[
            pl.BlockSpec((tb, d), lambda i: (i, 0)),
            pl.BlockSpec((1, d), const),
            pl.BlockSpec((d, 3 * d_b), const),
            pl.BlockSpec((conv_w.shape[0], d_b), const),
            pl.BlockSpec((d_b, d), const),
            pl.BlockSpec((tb, d_b), sample_blk),
            pl.BlockSpec((tb, d_b), sample_blk),
        ],
        out_specs=[
            pl.BlockSpec((tb, d), lambda i: (i, 0)),
            pl.BlockSpec((1, SUBLANES, d_b), lambda i: (i, 0, 0)),
            pl.BlockSpec((tb, d_b), sample_blk),
        ],
        out_shape=[
            jax.ShapeDtypeStruct((t, d), F32),
            jax.ShapeDtypeStruct((nb, SUBLANES, d_b), F32),
            jax.ShapeDtypeStruct((n_sample_blocks * tb, d_b), F32),
        ],
        scratch_shapes=[pltpu.VMEM((tb + SUBLANES, d_b), F32)],
        compiler_params=_params(1),
        name="mixer_b",
    )(x, g.reshape(1, d), w_in, conv_w, w_out, h1, h2)


def _merge_exchange_pairs(n):
    pairs = []
    p = 1
    while p < n:
        k = p
        while k >= 1:
            for j in range(k % p, n - k, 2 * k):
                for i in range(min(k, n - j - k)):
                    if (i + j) // (2 * p) == (i + j + k) // (2 * p):
                        pairs.append((i + j, i + j + k))
            k //= 2
        p *= 2
    return pairs


def _top_sorted(v):
    n = len(v)
    v = list(v)
    for i, j in _merge_exchange_pairs(n):
        v[i], v[j] = jnp.maximum(v[i], v[j]), jnp.minimum(v[i], v[j])
    shift = SUBLANES // 2
    while shift >= 1:
        v = [jnp.maximum(v[i], pltpu.roll(v[n - 1 - i], shift, 0)) for i in range(n)]
        stride = n // 2
        while stride >= 1:
            for i in range(n):
                if i & stride == 0:
                    v[i], v[i + stride] = (jnp.maximum(v[i], v[i + stride]),
                                           jnp.minimum(v[i], v[i + stride]))
            stride //= 2
        shift //= 2
    return v


def _top_values(s, n):
    out = []
    cur = s
    for _ in range(n):
        m = jnp.max(cur, axis=0, keepdims=True)
        out.append(m)
        cur = jnp.where(cur == m, NEG_INF, cur)
    return out


def _peer_item(n, even, x_ref, g_ref, wqt_ref, keys_ref, keyl_ref, u_ref, v_ref, xo_ref,
               tokt_ref, qth_ref, qtl_ref, sc1_ref, th_ref, s2_ref, e2_ref, c1_ref,
               at0_ref, at1_ref, mt0_ref, mt1_ref, acc_ref, *, n_items, groups_per_block):
    nk = groups_per_block
    tb = x_ref.shape[0]
    n_heads = keys_ref.shape[0]
    n_keys = keys_ref.shape[2]
    d_half = keys_ref.shape[3]
    n_lane_tiles = tb // LANES
    key_tiles = n_keys // SUBLANES

    def transpose_tokens():
        tok = _rms(x_ref[...], g_ref[...])
        tokt_ref[...] = tok.T.astype(BF16)

    def when(parity_ok, cond):
        return pl.when(cond) if parity_ok else (lambda f: f)

    @when(even, n == 0)
    def _init():
        for ref in (th_ref, s2_ref, e2_ref, c1_ref, at0_ref, at1_ref, mt0_ref, mt1_ref,
                    acc_ref):
            ref[...] = jnp.zeros(ref.shape, ref.dtype)
        transpose_tokens()

    @when(even, jnp.logical_and(n >= 2, (n - 2) % nk == 0))
    def _start_block():
        acc_ref[...] = x_ref[...]

    d_model = v_ref.shape[1]
    gate_tiles = n_lane_tiles * (n_keys // GATE_ROWS)
    mxu_n = 2 * LANES
    col_chunks = tb // mxu_n
    a_chunks = A_ROW_CHUNKS * col_chunks
    tok_chunks = tb // C_TOKEN_CHUNK
    c_chunks = tok_chunks * (d_model // C_COL_CHUNK)
    a_every = gate_tiles // a_chunks
    c_every = gate_tiles // c_chunks

    def stage_a(at_a, chunk):
        n_rows = (KEYS_PER_STEP * n_keys) // A_ROW_CHUNKS
        rows = slice((chunk // col_chunks) * n_rows, (chunk // col_chunks + 1) * n_rows)
        cols = slice((chunk % col_chunks) * mxu_n, (chunk % col_chunks + 1) * mxu_n)
        at_a[rows, cols] = jnp.dot(u_ref[rows, :], tokt_ref[:, cols], preferred_element_type=F32)

    def stage_c(mt_c, chunk):
        toks = slice((chunk % tok_chunks) * C_TOKEN_CHUNK, (chunk % tok_chunks + 1) * C_TOKEN_CHUNK)
        cols = slice((chunk // tok_chunks) * C_COL_CHUNK, (chunk // tok_chunks + 1) * C_COL_CHUNK)
        acc_ref[toks, cols] += lax.dot_general(mt_c[:, toks], v_ref[:, cols], (((0,), (0,)), ((), ())),
                                               preferred_element_type=F32)

    def stages(at_a, at_b, mt_b, mt_c):
        kb = (n + nk - 1) % nk
        tile = 0
        for j in range(n_lane_tiles):
            lanes = slice(j * LANES, (j + 1) * LANES)
            for r in range(n_keys // GATE_ROWS):
                if tile % a_every == 0:
                    stage_a(at_a, tile // a_every)
                if tile % c_every == min(a_every, c_every) // 2:
                    stage_c(mt_c, tile // c_every)
                tile += 1
                rows = slice(r * GATE_ROWS, (r + 1) * GATE_ROWS)
                for ii0 in range(0, KEYS_PER_STEP, GATE_KEYS):
                    w = [jnp.zeros((GATE_ROWS, LANES), F32)] * GATE_KEYS
                    for h in range(n_heads):
                        s2t = s2_ref[h, rows, lanes]
                        e2t = e2_ref[h, rows, lanes]
                        for ii in range(GATE_KEYS):
                            throw = th_ref[h, kb, ii0 + ii:ii0 + ii + 1, lanes]
                            c1row = c1_ref[h, kb, ii0 + ii:ii0 + ii + 1, lanes]
                            w[ii] = w[ii] + jnp.where(s2t >= throw, e2t, 0.0) * c1row
                    for ii in range(GATE_KEYS):
                        row0 = (ii0 + ii) * n_keys + r * GATE_ROWS
                        arow = slice(row0, row0 + GATE_ROWS)
                        a = at_b[arow, lanes].astype(BF16)
                        aw = a * w[ii].astype(BF16)
                        t = jnp.tanh(a * (GELU_K0 + GELU_K1 * (a * a)))
                        mt_b[arow, lanes] = aw + aw * t

    if even:
        stages(at0_ref, at1_ref, mt1_ref, mt0_ref)
    else:
        stages(at1_ref, at0_ref, mt0_ref, mt1_ref)

    @when(not even, jnp.logical_and(n >= 2, (n - 2) % nk == nk - 1))
    def _finish_block():
        xo_ref[...] = acc_ref[...]

    @when(not even, jnp.logical_and((n + 1) % nk == 0, n + 1 < n_items))
    def _next_tokens():
        transpose_tokens()

    @when(even, jnp.logical_and(n % nk == 0, n < n_items))
    def _route():
        qt = jnp.dot(wqt_ref[...], tokt_ref[...], preferred_element_type=F32)
        qt_hi = qt.astype(BF16)
        qth_ref[...] = qt_hi
        qtl_ref[...] = (qt - qt_hi.astype(F32)).astype(BF16)

        def head(h, carry):
            for p in range(2):
                q_rows = pl.ds(pl.multiple_of((2 * h + p) * d_half, d_half), d_half)
                q_hi = qth_ref[q_rows, :]
                s = (jnp.dot(keys_ref[h, p], q_hi, preferred_element_type=F32)
                     + jnp.dot(keys_ref[h, p], qtl_ref[q_rows, :], preferred_element_type=F32)
                     + jnp.dot(keyl_ref[h, p], q_hi, preferred_element_type=F32))
                if p == 0:
                    sc1_ref[...] = s.reshape(key_tiles, SUBLANES, tb)
                else:
                    s2_ref[h] = s
            sublane = lax.broadcasted_iota(jnp.int32, (SUBLANES, LANES), 0)

            def pack(rows):
                out = rows[-1]
                for i in range(len(rows) - 2, -1, -1):
                    out = jnp.where(sublane == i, rows[i], out)
                return out

            for j in range(n_lane_tiles):
                lanes = slice(j * LANES, (j + 1) * LANES)
                s1 = sc1_ref[:, :, lanes]
                s2 = s2_ref[h, :, lanes].reshape(key_tiles, SUBLANES, LANES)
                r1 = _top_sorted([s1[i] for i in range(key_tiles)])
                r2 = _top_sorted([s2[i] for i in range(key_tiles)])
                half = TOPK_HALF // 2
                sv2_lo, sv2_hi, sv1_hi = pack(r2[:half]), pack(r2[half:]), pack(r1[half:])
                cand = ([r1[0] + sv2_lo, r1[0] + sv2_hi] + [r1[a] + sv2_lo for a in range(1, half)]
                        + [sv1_hi + r2[0]])
                best = _top_values(jnp.concatenate(cand, axis=0), TOPK)
                zsum = jnp.zeros_like(best[0])
                for m in best:
                    zsum = zsum + jnp.exp(m - best[0])
                tau = jnp.broadcast_to(best[-1], (SUBLANES, LANES))
                th = jnp.full(s1.shape, jnp.inf, F32)
                th_top = jnp.full((SUBLANES, LANES), jnp.inf, F32)
                for b in range(TOPK_HALF):
                    if b < half:
                        th = jnp.where(s1 + r2[b][None] >= tau[None], r2[b][None], th)
                    th_top = jnp.where(r1[0] + r2[b] >= tau, r2[b], th_top)
                th_ref[h, :, :, lanes] = jnp.where(s1 == r1[0][None], th_top[None], th)
                scale = jnp.broadcast_to(0.5 / zsum, (SUBLANES, LANES))
                c1_ref[h, :, :, lanes] = jnp.exp(s1 - r1[0][None]) * scale[None]
                e2_ref[h, :, lanes] = jnp.exp(s2 - r2[0][None]).reshape(n_keys, LANES)
            return carry

        lax.fori_loop(0, n_heads, head, 0)


def _peer_kernel(x_ref, g_ref, wqt_ref, keys_ref, keyl_ref, u0_ref, u1_ref, v0_ref, v1_ref, xo_ref,
                 *scratch, n_items, groups_per_block):
    step = pl.program_id(0)
    shared = (x_ref, g_ref, wqt_ref, keys_ref, keyl_ref)
    sizes = dict(n_items=n_items, groups_per_block=groups_per_block)
    _peer_item(2 * step, True, *shared, u0_ref, v0_ref, xo_ref, *scratch, **sizes)
    _peer_item(2 * step + 1, False, *shared, u1_ref, v1_ref, xo_ref, *scratch, **sizes)


def _peer(x, g, wq_t, keys, u_tabs, v_tabs, layer):
    t, d = x.shape
    tb = TOKEN_BLOCK
    n_heads, _, n_keys, d_half = keys.shape
    n_experts = u_tabs.shape[1]
    eb = KEYS_PER_STEP * n_keys
    nk = n_experts // eb
    n_blocks = t // tb
    n_items = n_blocks * nk
    assert nk >= 8 and nk % 2 == 0 and n_keys == TOPK_HALF * SUBLANES
    keys_hi = keys.astype(BF16)
    keys_lo = (keys - keys_hi.astype(F32)).astype(BF16)
    kern = functools.partial(_peer_kernel, n_items=n_items, groups_per_block=nk)
    key_tiles = n_keys // SUBLANES
    u_rows = lambda n: jnp.minimum(n, n_items - 1) % nk
    v_rows = lambda n: jnp.maximum(n - 2, 0) % nk
    return pl.pallas_call(
        kern,
        grid=((n_items + 2) // 2,),
        in_specs=[
            pl.BlockSpec((tb, d), lambda s: (jnp.minimum((2 * s + 2) // nk, n_blocks - 1), 0)),
            pl.BlockSpec((1, d), lambda s: (0, 0)),
            pl.BlockSpec(wq_t.shape, lambda s: (0, 0)),
            pl.BlockSpec(keys.shape, lambda s: (0, 0, 0, 0)),
            pl.BlockSpec(keys.shape, lambda s: (0, 0, 0, 0)),
            pl.BlockSpec((None, eb, d), lambda s: (layer, u_rows(2 * s), 0)),
            pl.BlockSpec((None, eb, d), lambda s: (layer, u_rows(2 * s + 1), 0)),
            pl.BlockSpec((None, eb, d), lambda s: (layer, v_rows(2 * s), 0)),
            pl.BlockSpec((None, eb, d), lambda s: (layer, v_rows(2 * s + 1), 0)),
        ],
        out_specs=pl.BlockSpec((tb, d), lambda s: (jnp.maximum(2 * s - 1, 0) // nk, 0)),
        out_shape=jax.ShapeDtypeStruct((t, d), F32),
        scratch_shapes=[
            pltpu.VMEM((d, tb), BF16),
            pltpu.VMEM((wq_t.shape[0], tb), BF16),
            pltpu.VMEM((wq_t.shape[0], tb), BF16),
            pltpu.VMEM((key_tiles, SUBLANES, tb), F32),
            pltpu.VMEM((n_heads, key_tiles, SUBLANES, tb), F32),
            pltpu.VMEM((n_heads, n_keys, tb), F32),
            pltpu.VMEM((n_heads, n_keys, tb), F32),
            pltpu.VMEM((n_heads, key_tiles, SUBLANES, tb), F32),
            pltpu.VMEM((eb, tb), F32),
            pltpu.VMEM((eb, tb), F32),
            pltpu.VMEM((eb, tb), BF16),
            pltpu.VMEM((eb, tb), BF16),
            pltpu.VMEM((tb, d), F32),
        ],
        compiler_params=_params(1),
        name="peer",
    )(x, g.reshape(1, d), wq_t, keys_hi, keys_lo, u_tabs, u_tabs, v_tabs, v_tabs)


def _final_norm_kernel(x_ref, g_ref, yp_ref, ys_ref, *, n_prompt_blocks):
    i = pl.program_id(0)
    y = _rms(x_ref[...], g_ref[...])

    @pl.when(i < n_prompt_blocks)
    def _():
        yp_ref[...] = y

    @pl.when(i >= n_prompt_blocks)
    def _():
        ys_ref[...] = y


def _final_norm(x, g, n_prompt_blocks):
    t, d = x.shape
    tb = TOKEN_BLOCK
    nb = t // tb
    return pl.pallas_call(
        functools.partial(_final_norm_kernel, n_prompt_blocks=n_prompt_blocks),
        grid=(nb,),
        in_specs=[pl.BlockSpec((tb, d), lambda i: (i, 0)), pl.BlockSpec((1, d), lambda i: (0, 0))],
        out_specs=[
            pl.BlockSpec((tb, d), lambda i: (jnp.minimum(i, n_prompt_blocks - 1), 0)),
            pl.BlockSpec((tb, d), lambda i: (jnp.maximum(i - n_prompt_blocks, 0), 0)),
        ],
        out_shape=[
            jax.ShapeDtypeStruct((n_prompt_blocks * tb, d), F32),
            jax.ShapeDtypeStruct(((nb - n_prompt_blocks) * tb, d), F32),
        ],
        compiler_params=_params(1),
        name="final_norm",
    )(x, g.reshape(1, d))


def _spatial_weights(w_s, b_s, sample_len, d_a):
    n_g = w_s.shape[0]
    d_g = d_a // n_g
    tril = lambda n: jnp.tril(jnp.ones((n, n), dtype=bool))
    ws_p = jnp.where(tril(CHUNK)[None], w_s, 0)
    ws4 = jnp.where(tril(sample_len)[None], w_s[:, :sample_len, :sample_len], 0)
    reps = CHUNK // sample_len
    eye = jnp.eye(reps, dtype=w_s.dtype)
    ws_s = (eye[None, :, None, :, None] * ws4[:, None, :, None, :]).reshape(n_g, CHUNK, CHUNK)
    ws2 = jnp.stack([ws_p, ws_s]).astype(BF16)
    bias_p = jnp.repeat(jnp.transpose(b_s), d_g, axis=1)
    bias_s = jnp.repeat(jnp.tile(jnp.transpose(b_s[:, :sample_len]), (reps, 1)), d_g, axis=1)
    return ws2, jnp.stack([bias_p, bias_s]).astype(F32)


def kernel(x_prompt, x_sample, state_conv, norm_mix, norm_ffn, norm_final, a_w_in, a_ln_g, a_ln_b,
           a_w_s, a_b_s, a_w_out, b_w_in, b_conv_w, b_w_out, peer_w_query, peer_sub_keys,
           peer_expert_u, peer_expert_v):
    batch, seq, d = x_prompt.shape
    n_seq_s, sample_len, _ = x_sample.shape
    depth = norm_mix.shape[0]
    d_a = a_ln_g.shape[1]
    d_b = b_w_out.shape[1]
    conv_taps = b_conv_w.shape[1]
    tb = TOKEN_BLOCK
    n_prompt = batch * seq
    n_sample = n_seq_s * sample_len
    assert seq % tb == 0 and n_sample % tb == 0 and tb % CHUNK == 0
    assert CHUNK % sample_len == 0 and sample_len >= conv_taps - 1 and conv_taps == 3
    assert peer_sub_keys.shape[3] % (8 * SUBLANES) == 0
    n_prompt_blocks = n_prompt // tb

    x = jnp.concatenate([x_prompt.reshape(n_prompt, d), x_sample.reshape(n_sample, d)], axis=0)
    v_rows, conv_prompt, conv_sample = [], [], []
    expert_u = peer_expert_u.astype(BF16)
    expert_v = peer_expert_v.astype(BF16)
    for i in range(depth):
        j = i // 2
        if i % 2 == 0:
            ws2, bias2 = _spatial_weights(a_w_s[j], a_b_s[j], sample_len, d_a)
            x, v = _mixer_a(x, norm_mix[i], a_w_in[j].astype(BF16), a_ln_g[j], a_ln_b[j], ws2, bias2,
                            a_w_out[j].astype(BF16), n_prompt_blocks)
            v_rows.append(v.reshape(n_seq_s, sample_len, d_a))
        else:
            buf = state_conv[j]
            pad = lambda rows: jnp.concatenate(
                [rows, jnp.zeros((n_seq_s, sample_len - rows.shape[1], d_b), F32)], axis=1
            ).reshape(n_sample, d_b)
            h1 = pad(buf[:, 1:2])
            h2 = pad(buf[:, 0:2])
            x, zlast, zs = _mixer_b(x, norm_mix[i], b_w_in[j].astype(BF16), b_conv_w[j],
                                    b_w_out[j].astype(BF16), h1, h2, n_prompt_blocks, seq // tb,
                                    sample_len)
            blocks_per_seq = seq // tb
            conv_prompt.append(zlast[blocks_per_seq - 1:n_prompt_blocks:blocks_per_seq, SUBLANES - 2:])
            conv_sample.append(zs.reshape(n_seq_s, sample_len, d_b)[:, sample_len - 2:])
        x = _peer(x, norm_ffn[i], jnp.transpose(peer_w_query[i]).astype(BF16), peer_sub_keys[i],
                  expert_u, expert_v, i)
    y_prompt, y_sample = _final_norm(x, norm_final, n_prompt_blocks)
    return (y_prompt.reshape(batch, seq, d), y_sample.reshape(n_seq_s, sample_len, d),
            jnp.stack(v_rows), jnp.stack(conv_prompt), jnp.stack(conv_sample))
```

```python
import functools

import jax
import jax.numpy as jnp
from jax import lax
from jax.experimental import pallas as pl
from jax.experimental.pallas import tpu as pltpu

F32 = jnp.float32
BF16 = jnp.bfloat16

EPS = 1e-6
CHUNK = 128
N_GROUPS_A = 8
TOPK_HALF = 16
TOPK = 16
LANES = 128
SUBLANES = 8
TOKEN_BLOCK = 512
KEYS_PER_STEP = SUBLANES
GATE_ROWS = 2 * SUBLANES
GATE_KEYS = KEYS_PER_STEP
A_ROW_CHUNKS = 8
C_TOKEN_CHUNK = 512
C_COL_CHUNK = 256
VMEM_LIMIT = 60 * 1024 * 1024
GELU_K0 = 0.7978845608028654
GELU_K1 = GELU_K0 * 0.044715
NEG_INF = float("-inf")


def _rms(x, g):
    return x * lax.rsqrt(jnp.mean(x * x, axis=-1, keepdims=True) + EPS) * g


def _params(n_axes):
    return pltpu.CompilerParams(dimension_semantics=("arbitrary",) * n_axes,
                                vmem_limit_bytes=VMEM_LIMIT)


def _mixer_a_kernel(x_ref, g_ref, win_ref, lng_ref, lnb_ref, ws_ref, bias_ref, wout_ref,
                    xo_ref, v_ref, gated_ref):
    tb = x_ref.shape[0]
    d_a = lng_ref.shape[1]
    d_g = d_a // N_GROUPS_A
    x = x_ref[...]
    h = _rms(x, g_ref[...]).astype(BF16)
    z = jax.nn.gelu(jnp.dot(h, win_ref[...], preferred_element_type=F32))
    v = z[:, d_a:]
    mu = jnp.mean(v, axis=-1, keepdims=True)
    vc = v - mu
    var = jnp.mean(vc * vc, axis=-1, keepdims=True)
    vn = vc * lax.rsqrt(var + EPS) * lng_ref[...] + lnb_ref[...]
    v_ref[...] = vn
    vb = vn.astype(BF16)
    for c in range(tb // CHUNK):
        rows = slice(c * CHUNK, (c + 1) * CHUNK)
        for g in range(N_GROUPS_A):
            cols = slice(g * d_g, (g + 1) * d_g)
            mixed = jnp.dot(ws_ref[0, g], vb[rows, cols], preferred_element_type=F32)
            mixed = mixed + bias_ref[0, :, cols]
            gated_ref[rows, cols] = (z[rows, cols] * mixed).astype(BF16)
    y = jnp.dot(gated_ref[...], wout_ref[...], preferred_element_type=F32)
    xo_ref[...] = x + y


def _mixer_a(x, g, w_in, ln_g, ln_b, ws2, bias2, w_out, n_prompt_blocks):
    t, d = x.shape
    tb = TOKEN_BLOCK
    d_a = ln_g.shape[0]
    nb = t // tb
    n_sample_blocks = nb - n_prompt_blocks
    is_sample = lambda i: jnp.where(i >= n_prompt_blocks, 1, 0)
    const = lambda i: (0, 0)
    return pl.pallas_call(
        _mixer_a_kernel,
        grid=(nb,),
        in_specs=[
            pl.BlockSpec((tb, d), lambda i: (i, 0)),
            pl.BlockSpec((1, d), const),
            pl.BlockSpec((d, 2 * d_a), const),
            pl.BlockSpec((1, d_a), const),
            pl.BlockSpec((1, d_a), const),
            pl.BlockSpec((1, N_GROUPS_A, CHUNK, CHUNK), lambda i: (is_sample(i), 0, 0, 0)),
            pl.BlockSpec((1, CHUNK, d_a), lambda i: (is_sample(i), 0, 0)),
            pl.BlockSpec((d_a, d), const),
        ],
        out_specs=[
            pl.BlockSpec((tb, d), lambda i: (i, 0)),
            pl.BlockSpec((tb, d_a), lambda i: (jnp.maximum(i - n_prompt_blocks, 0), 0)),
        ],
        out_shape=[
            jax.ShapeDtypeStruct((t, d), F32),
            jax.ShapeDtypeStruct((n_sample_blocks * tb, d_a), F32),
        ],
        scratch_shapes=[pltpu.VMEM((tb, d_a), BF16)],
        compiler_params=_params(1),
        name="mixer_a",
    )(x, g.reshape(1, d), w_in, ln_g.reshape(1, d_a), ln_b.reshape(1, d_a), ws2, bias2, w_out)


def _mixer_b_kernel(x_ref, g_ref, win_ref, cw_ref, wout_ref, h1_ref, h2_ref,
                    xo_ref, zlast_ref, zs_ref, zext_ref, *, n_prompt_blocks, blocks_per_seq,
                    sample_len):
    i = pl.program_id(0)
    tb = x_ref.shape[0]
    d_b = wout_ref.shape[0]
    halo = SUBLANES
    x = x_ref[...]
    h = _rms(x, g_ref[...]).astype(BF16)
    bcz = jnp.dot(h, win_ref[...], preferred_element_type=F32)
    b_gate = bcz[:, :d_b]
    z = bcz[:, d_b:2 * d_b] * bcz[:, 2 * d_b:]
    is_sample = i >= n_prompt_blocks

    @pl.when(jnp.logical_or(is_sample, i % blocks_per_seq == 0))
    def _():
        zext_ref[0:halo, :] = jnp.zeros((halo, d_b), F32)

    zext_ref[halo:, :] = z
    z1 = zext_ref[halo - 1:halo - 1 + tb, :]
    z2 = zext_ref[halo - 2:halo - 2 + tb, :]
    pos = lax.broadcasted_iota(jnp.int32, (tb, 1), 0) % sample_len
    z1 = jnp.where(jnp.logical_and(is_sample, pos < 1), h1_ref[...], z1)
    z2 = jnp.where(jnp.logical_and(is_sample, pos < 2), h2_ref[...], z2)
    conv = cw_ref[0:1, :] * z2 + cw_ref[1:2, :] * z1 + cw_ref[2:3, :] * z
    y = jnp.dot((b_gate * conv).astype(BF16), wout_ref[...], preferred_element_type=F32)
    xo_ref[...] = x + y
    tail = z[tb - halo:, :]
    zlast_ref[0] = tail
    zs_ref[...] = z
    zext_ref[0:halo, :] = tail


def _mixer_b(x, g, w_in, conv_w, w_out, h1, h2, n_prompt_blocks, blocks_per_seq, sample_len):
    t, d = x.shape
    tb = TOKEN_BLOCK
    d_b = w_out.shape[0]
    nb = t // tb
    n_sample_blocks = nb - n_prompt_blocks
    const = lambda i: (0, 0)
    sample_blk = lambda i: (jnp.maximum(i - n_prompt_blocks, 0), 0)
    kern = functools.partial(_mixer_b_kernel, n_prompt_blocks=n_prompt_blocks,
                             blocks_per_seq=blocks_per_seq, sample_len=sample_len)
    return pl.pallas_call(
        kern,
        grid=(nb,),
        in_specs=[
            pl.BlockSpec((tb, d), lambda i: (i, 0)),
            pl.BlockSpec((1, d), const),
            pl.BlockSpec((d, 3 * d_b), const),
            pl.BlockSpec((conv_w.shape[0], d_b), const),
            pl.BlockSpec((d_b, d), const),
            pl.BlockSpec((tb, d_b), sample_blk),
            pl.BlockSpec((tb, d_b), sample_blk),
        ],
        out_specs=[
            pl.BlockSpec((tb, d), lambda i: (i, 0)),
            pl.BlockSpec((1, SUBLANES, d_b), lambda i: (i, 0, 0)),
            pl.BlockSpec((tb, d_b), sample_blk),
        ],
        out_shape=[
            jax.ShapeDtypeStruct((t, d), F32),
            jax.ShapeDtypeStruct((nb, SUBLANES, d_b), F32),
            jax.ShapeDtypeStruct((n_sample_blocks * tb, d_b), F32),
        ],
        scratch_shapes=[pltpu.VMEM((tb + SUBLANES, d_b), F32)],
        compiler_params=_params(1),
        name="mixer_b",
    )(x, g.reshape(1, d), w_in, conv_w, w_out, h1, h2)


def _merge_exchange_pairs(n):
    pairs = []
    p = 1
    while p < n:
        k = p
        while k >= 1:
            for j in range(k % p, n - k, 2 * k):
                for i in range(min(k, n - j - k)):
                    if (i + j) // (2 * p) == (i + j + k) // (2 * p):
                        pairs.append((i + j, i + j + k))
            k //= 2
        p *= 2
    return pairs


def _top_sorted(v):
    n = len(v)
    v = list(v)
    for i, j in _merge_exchange_pairs(n):
        v[i], v[j] = jnp.maximum(v[i], v[j]), jnp.minimum(v[i], v[j])
    shift = SUBLANES // 2
    while shift >= 1:
        v = [jnp.maximum(v[i], pltpu.roll(v[n - 1 - i], shift, 0)) for i in range(n)]
        stride = n // 2
        while stride >= 1:
            for i in range(n):
                if i & stride == 0:
                    v[i], v[i + stride] = (jnp.maximum(v[i], v[i + stride]),
                                           jnp.minimum(v[i], v[i + stride]))
            stride //= 2
        shift //= 2
    return v


def _top_values(s, n):
    out = []
    cur = s
    for _ in range(n):
        m = jnp.max(cur, axis=0, keepdims=True)
        out.append(m)
        cur = jnp.where(cur == m, NEG_INF, cur)
    return out


def _peer_item(n, even, x_ref, g_ref, wqt_ref, keys_ref, keyl_ref, u_ref, v_ref, xo_ref,
               tokt_ref, qth_ref, qtl_ref, sc1_ref, th_ref, s2_ref, e2_ref, c1_ref,
               at0_ref, at1_ref, mt0_ref, mt1_ref, *, n_items, groups_per_block):
    nk = groups_per_block
    tb = x_ref.shape[0]
    n_heads = keys_ref.shape[0]
    n_keys = keys_ref.shape[2]
    d_half = keys_ref.shape[3]
    n_lane_tiles = tb // LANES
    key_tiles = n_keys // SUBLANES

    def transpose_tokens():
        tok = _rms(x_ref[...], g_ref[...])
        tokt_ref[...] = tok.T.astype(BF16)

    def when(parity_ok, cond):
        return pl.when(cond) if parity_ok else (lambda f: f)

    @when(even, n == 0)
    def _init():
        for ref in (th_ref, s2_ref, e2_ref, c1_ref, at0_ref, at1_ref, mt0_ref, mt1_ref,
                    xo_ref):
            ref[...] = jnp.zeros(ref.shape, ref.dtype)
        transpose_tokens()

    @when(even, jnp.logical_and(n >= 2, (n - 2) % nk == 0))
    def _start_block():
        xo_ref[...] = x_ref[...]

    d_model = v_ref.shape[1]
    gate_tiles = n_lane_tiles * (n_keys // GATE_ROWS)
    mxu_n = 2 * LANES
    col_chunks = tb // mxu_n
    a_chunks = A_ROW_CHUNKS * col_chunks
    tok_chunks = tb // C_TOKEN_CHUNK
    c_chunks = tok_chunks * (d_model // C_COL_CHUNK)
    a_every = gate_tiles // a_chunks
    c_every = gate_tiles // c_chunks

    def stage_a(at_a, chunk):
        n_rows = (KEYS_PER_STEP * n_keys) // A_ROW_CHUNKS
        rows = slice((chunk // col_chunks) * n_rows, (chunk // col_chunks + 1) * n_rows)
        cols = slice((chunk % col_chunks) * mxu_n, (chunk % col_chunks + 1) * mxu_n)
        at_a[rows, cols] = jnp.dot(u_ref[rows, :], tokt_ref[:, cols], preferred_element_type=F32)

    def stage_c(mt_c, chunk):
        toks = slice((chunk % tok_chunks) * C_TOKEN_CHUNK, (chunk % tok_chunks + 1) * C_TOKEN_CHUNK)
        cols = slice((chunk // tok_chunks) * C_COL_CHUNK, (chunk // tok_chunks + 1) * C_COL_CHUNK)
        xo_ref[toks, cols] += lax.dot_general(mt_c[:, toks], v_ref[:, cols], (((0,), (0,)), ((), ())),
                                               preferred_element_type=F32)

    def stages(at_a, at_b, mt_b, mt_c):
        kb = (n + nk - 1) % nk
        tile = 0
        for j in range(n_lane_tiles):
            lanes = slice(j * LANES, (j + 1) * LANES)
            for r in range(n_keys // GATE_ROWS):
                if tile % a_every == 0:
                    stage_a(at_a, tile // a_every)
                if tile % c_every == min(a_every, c_every) // 2:
                    stage_c(mt_c, tile // c_every)
                tile += 1
                rows = slice(r * GATE_ROWS, (r + 1) * GATE_ROWS)
                for ii0 in range(0, KEYS_PER_STEP, GATE_KEYS):
                    w = [jnp.zeros((GATE_ROWS, LANES), F32)] * GATE_KEYS
                    for h in range(n_heads):
                        s2t = s2_ref[h, rows, lanes]
                        e2t = e2_ref[h, rows, lanes]
                        for ii in range(GATE_KEYS):
                            throw = th_ref[h, kb, ii0 + ii:ii0 + ii + 1, lanes]
                            c1row = c1_ref[h, kb, ii0 + ii:ii0 + ii + 1, lanes]
                            w[ii] = w[ii] + jnp.where(s2t >= throw, e2t, 0.0) * c1row
                    for ii in range(GATE_KEYS):
                        row0 = (ii0 + ii) * n_keys + r * GATE_ROWS
                        arow = slice(row0, row0 + GATE_ROWS)
                        a = at_b[arow, lanes].astype(BF16)
                        aw = a * w[ii].astype(BF16)
                        t = jnp.tanh(a * (GELU_K0 + GELU_K1 * (a * a)))
                        mt_b[arow, lanes] = aw + aw * t

    if even:
        stages(at0_ref, at1_ref, mt1_ref, mt0_ref)
    else:
        stages(at1_ref, at0_ref, mt0_ref, mt1_ref)

    @when(not even, jnp.logical_and((n + 1) % nk == 0, n + 1 < n_items))
    def _next_tokens():
        transpose_tokens()

    @when(even, jnp.logical_and(n % nk == 0, n < n_items))
    def _route():
        qt = jnp.dot(wqt_ref[...], tokt_ref[...], preferred_element_type=F32)
        qt_hi = qt.astype(BF16)
        qth_ref[...] = qt_hi
        qtl_ref[...] = (qt - qt_hi.astype(F32)).astype(BF16)

        def head(h, carry):
            for p in range(2):
                q_rows = pl.ds(pl.multiple_of((2 * h + p) * d_half, d_half), d_half)
                q_hi = qth_ref[q_rows, :]
                s = (jnp.dot(keys_ref[h, p], q_hi, preferred_element_type=F32)
                     + jnp.dot(keys_ref[h, p], qtl_ref[q_rows, :], preferred_element_type=F32)
                     + jnp.dot(keyl_ref[h, p], q_hi, preferred_element_type=F32))
                if p == 0:
                    sc1_ref[...] = s.reshape(key_tiles, SUBLANES, tb)
                else:
                    s2_ref[h] = s
            sublane = lax.broadcasted_iota(jnp.int32, (SUBLANES, LANES), 0)

            def pack(rows):
                out = rows[-1]
                for i in range(len(rows) - 2, -1, -1):
                    out = jnp.where(sublane == i, rows[i], out)
                return out

            for j in range(n_lane_tiles):
                lanes = slice(j * LANES, (j + 1) * LANES)
                s1 = sc1_ref[:, :, lanes]
                s2 = s2_ref[h, :, lanes].reshape(key_tiles, SUBLANES, LANES)
                r1 = _top_sorted([s1[i] for i in range(key_tiles)])
                r2 = _top_sorted([s2[i] for i in range(key_tiles)])
                half = TOPK_HALF // 2
                sv2_lo, sv2_hi, sv1_hi = pack(r2[:half]), pack(r2[half:]), pack(r1[half:])
                cand = ([r1[0] + sv2_lo, r1[0] + sv2_hi] + [r1[a] + sv2_lo for a in range(1, half)]
                        + [sv1_hi + r2[0]])
                best = _top_values(jnp.concatenate(cand, axis=0), TOPK)
                zsum = jnp.zeros_like(best[0])
                for m in best:
                    zsum = zsum + jnp.exp(m - best[0])
                tau = jnp.broadcast_to(best[-1], (SUBLANES, LANES))
                th = jnp.full(s1.shape, jnp.inf, F32)
                th_top = jnp.full((SUBLANES, LANES), jnp.inf, F32)
                for b in range(TOPK_HALF):
                    if b < half:
                        th = jnp.where(s1 + r2[b][None] >= tau[None], r2[b][None], th)
                    th_top = jnp.where(r1[0] + r2[b] >= tau, r2[b], th_top)
                th_ref[h, :, :, lanes] = jnp.where(s1 == r1[0][None], th_top[None], th)
                scale = jnp.broadcast_to(0.5 / zsum, (SUBLANES, LANES))
                c1_ref[h, :, :, lanes] = jnp.exp(s1 - r1[0][None]) * scale[None]
                e2_ref[h, :, lanes] = jnp.exp(s2 - r2[0][None]).reshape(n_keys, LANES)
            return carry

        lax.fori_loop(0, n_heads, head, 0)


def _peer_kernel(x_ref, g_ref, wqt_ref, keys_ref, keyl_ref, u0_ref, u1_ref, v0_ref, v1_ref, xo_ref,
                 *scratch, n_items, groups_per_block):
    step = pl.program_id(0)
    shared = (x_ref, g_ref, wqt_ref, keys_ref, keyl_ref)
    sizes = dict(n_items=n_items, groups_per_block=groups_per_block)
    _peer_item(2 * step, True, *shared, u0_ref, v0_ref, xo_ref, *scratch, **sizes)
    _peer_item(2 * step + 1, False, *shared, u1_ref, v1_ref, xo_ref, *scratch, **sizes)


def _peer(x, g, wq_t, keys, u_tabs, v_tabs, layer):
    t, d = x.shape
    tb = TOKEN_BLOCK
    n_heads, _, n_keys, d_half = keys.shape
    n_experts = u_tabs.shape[1]
    eb = KEYS_PER_STEP * n_keys
    nk = n_experts // eb
    n_blocks = t // tb
    n_items = n_blocks * nk
    assert nk >= 8 and nk % 2 == 0 and n_keys == TOPK_HALF * SUBLANES
    keys_hi = keys.astype(BF16)
    keys_lo = (keys - keys_hi.astype(F32)).astype(BF16)
    kern = functools.partial(_peer_kernel, n_items=n_items, groups_per_block=nk)
    key_tiles = n_keys // SUBLANES
    u_rows = lambda n: jnp.minimum(n, n_items - 1) % nk
    v_rows = lambda n: jnp.maximum(n - 2, 0) % nk
    return pl.pallas_call(
        kern,
        grid=((n_items + 2) // 2,),
        in_specs=[
            pl.BlockSpec((tb, d), lambda s: (jnp.minimum((2 * s + 2) // nk, n_blocks - 1), 0)),
            pl.BlockSpec((1, d), lambda s: (0, 0)),
            pl.BlockSpec(wq_t.shape, lambda s: (0, 0)),
            pl.BlockSpec(keys.shape, lambda s: (0, 0, 0, 0)),
            pl.BlockSpec(keys.shape, lambda s: (0, 0, 0, 0)),
            pl.BlockSpec((None, eb, d), lambda s: (layer, u_rows(2 * s), 0)),
            pl.BlockSpec((None, eb, d), lambda s: (layer, u_rows(2 * s + 1), 0)),
            pl.BlockSpec((None, eb, d), lambda s: (layer, v_rows(2 * s), 0)),
            pl.BlockSpec((None, eb, d), lambda s: (layer, v_rows(2 * s + 1), 0)),
        ],
        out_specs=pl.BlockSpec((tb, d), lambda s: (jnp.maximum(2 * s - 1, 0) // nk, 0)),
        out_shape=jax.ShapeDtypeStruct((t, d), F32),
        scratch_shapes=[
            pltpu.VMEM((d, tb), BF16),
            pltpu.VMEM((wq_t.shape[0], tb), BF16),
            pltpu.VMEM((wq_t.shape[0], tb), BF16),
            pltpu.VMEM((key_tiles, SUBLANES, tb), F32),
            pltpu.VMEM((n_heads, key_tiles, SUBLANES, tb), F32),
            pltpu.VMEM((n_heads, n_keys, tb), F32),
            pltpu.VMEM((n_heads, n_keys, tb), F32),
            pltpu.VMEM((n_heads, key_tiles, SUBLANES, tb), F32),
            pltpu.VMEM((eb, tb), F32),
            pltpu.VMEM((eb, tb), F32),
            pltpu.VMEM((eb, tb), BF16),
            pltpu.VMEM((eb, tb), BF16),
        ],
        compiler_params=_params(1),
        name="peer",
    )(x, g.reshape(1, d), wq_t, keys_hi, keys_lo, u_tabs, u_tabs, v_tabs, v_tabs)


def _final_norm_kernel(x_ref, g_ref, yp_ref, ys_ref, *, n_prompt_blocks):
    i = pl.program_id(0)
    y = _rms(x_ref[...], g_ref[...])

    @pl.when(i < n_prompt_blocks)
    def _():
        yp_ref[...] = y

    @pl.when(i >= n_prompt_blocks)
    def _():
        ys_ref[...] = y


def _final_norm(x, g, n_prompt_blocks):
    t, d = x.shape
    tb = TOKEN_BLOCK
    nb = t // tb
    return pl.pallas_call(
        functools.partial(_final_norm_kernel, n_prompt_blocks=n_prompt_blocks),
        grid=(nb,),
        in_specs=[pl.BlockSpec((tb, d), lambda i: (i, 0)), pl.BlockSpec((1, d), lambda i: (0, 0))],
        out_specs=[
            pl.BlockSpec((tb, d), lambda i: (jnp.minimum(i, n_prompt_blocks - 1), 0)),
            pl.BlockSpec((tb, d), lambda i: (jnp.maximum(i - n_prompt_blocks, 0), 0)),
        ],
        out_shape=[
            jax.ShapeDtypeStruct((n_prompt_blocks * tb, d), F32),
            jax.ShapeDtypeStruct(((nb - n_prompt_blocks) * tb, d), F32),
        ],
        compiler_params=_params(1),
        name="final_norm",
    )(x, g.reshape(1, d))


def _spatial_weights(w_s, b_s, sample_len, d_a):
    n_g = w_s.shape[0]
    d_g = d_a // n_g
    tril = lambda n: jnp.tril(jnp.ones((n, n), dtype=bool))
    ws_p = jnp.where(tril(CHUNK)[None], w_s, 0)
    ws4 = jnp.where(tril(sample_len)[None], w_s[:, :sample_len, :sample_len], 0)
    reps = CHUNK // sample_len
    eye = jnp.eye(reps, dtype=w_s.dtype)
    ws_s = (eye[None, :, None, :, None] * ws4[:, None, :, None, :]).reshape(n_g, CHUNK, CHUNK)
    ws2 = jnp.stack([ws_p, ws_s]).astype(BF16)
    bias_p = jnp.repeat(jnp.transpose(b_s), d_g, axis=1)
    bias_s = jnp.repeat(jnp.tile(jnp.transpose(b_s[:, :sample_len]), (reps, 1)), d_g, axis=1)
    return ws2, jnp.stack([bias_p, bias_s]).astype(F32)


def kernel(x_prompt, x_sample, state_conv, norm_mix, norm_ffn, norm_final, a_w_in, a_ln_g, a_ln_b,
           a_w_s, a_b_s, a_w_out, b_w_in, b_conv_w, b_w_out, peer_w_query, peer_sub_keys,
           peer_expert_u, peer_expert_v):
    batch, seq, d = x_prompt.shape
    n_seq_s, sample_len, _ = x_sample.shape
    depth = norm_mix.shape[0]
    d_a = a_ln_g.shape[1]
    d_b = b_w_out.shape[1]
    conv_taps = b_conv_w.shape[1]
    tb = TOKEN_BLOCK
    n_prompt = batch * seq
    n_sample = n_seq_s * sample_len
    assert seq % tb == 0 and n_sample % tb == 0 and tb % CHUNK == 0
    assert CHUNK % sample_len == 0 and sample_len >= conv_taps - 1 and conv_taps == 3
    assert peer_sub_keys.shape[3] % (8 * SUBLANES) == 0
    n_prompt_blocks = n_prompt // tb

    x = jnp.concatenate([x_prompt.reshape(n_prompt, d), x_sample.reshape(n_sample, d)], axis=0)
    v_rows, conv_prompt, conv_sample = [], [], []
    expert_u = peer_expert_u.astype(BF16)
    expert_v = peer_expert_v.astype(BF16)
    for i in range(depth):
        j = i // 2
        if i % 2 == 0:
            ws2, bias2 = _spatial_weights(a_w_s[j], a_b_s[j], sample_len, d_a)
            x, v = _mixer_a(x, norm_mix[i], a_w_in[j].astype(BF16), a_ln_g[j], a_ln_b[j], ws2, bias2,
                            a_w_out[j].astype(BF16), n_prompt_blocks)
            v_rows.append(v.reshape(n_seq_s, sample_len, d_a))
        else:
            buf = state_conv[j]
            pad = lambda rows: jnp.concatenate(
                [rows, jnp.zeros((n_seq_s, sample_len - rows.shape[1], d_b), F32)], axis=1
            ).reshape(n_sample, d_b)
            h1 = pad(buf[:, 1:2])
            h2 = pad(buf[:, 0:2])
            x, zlast, zs = _mixer_b(x, norm_mix[i], b_w_in[j].astype(BF16), b_conv_w[j],
                                    b_w_out[j].astype(BF16), h1, h2, n_prompt_blocks, seq // tb,
                                    sample_len)
            blocks_per_seq = seq // tb
            conv_prompt.append(zlast[blocks_per_seq - 1:n_prompt_blocks:blocks_per_seq, SUBLANES - 2:])
            conv_sample.append(zs.reshape(n_seq_s, sample_len, d_b)[:, sample_len - 2:])
        x = _peer(x, norm_ffn[i], jnp.transpose(peer_w_query[i]).astype(BF16), peer_sub_keys[i],
                  expert_u, expert_v, i)
    y_prompt, y_sample = _final_norm(x, norm_final, n_prompt_blocks)
    return (y_prompt.reshape(batch, seq, d), y_sample.reshape(n_seq_s, sample_len, d),
            jnp.stack(v_rows), jnp.stack(conv_prompt), jnp.stack(conv_sample))
```
